```python
import jax, jax.numpy as jnp
from jax import lax
import numpy as np

D_MODEL = 2048
BATCH = 2
SEQ = 4096
DEPTH = 4
DEC_BATCH = 128
DEC_SEQ = 8
PAST_LEN = 8192
PAGE_SIZE = 128

N_MIXERS = 3
HEAD_DIM = 128
N_HEADS = D_MODEL // HEAD_DIM
SB_KV_HEADS = 2
MOBA_KV_HEADS = 2
MOBA_BLOCK = 256
MOBA_TOPK = 3
ROPE_THETA = 10000.0
MLA_Q_LORA = 512
MLA_KV_LORA = 512
MLA_NOPE = 128
MLA_ROPE = 64
MLA_V = 128
D_FF = 5632
N_EXPERTS = 8
TOP_K = 2
D_FF_EXPERT = 2816
Q_BLOCK = 128
EPS = 1e-6
NEG = -1e30

N_SB = (DEPTH + 2) // 3
N_MLA = (DEPTH + 1) // 3
N_MOBA = DEPTH // 3
N_DENSE = (DEPTH + 1) // 2
N_MOE = DEPTH // 2

kernel_name = "hybrid_sb_mla_moba_adaln_step"


def rms_norm(x, g):
    xf = x.astype(jnp.float32)
    y = xf * lax.rsqrt(jnp.mean(xf * xf, axis=-1, keepdims=True) + EPS)
    return (y * g.astype(jnp.float32)).astype(x.dtype)


def modulation(c, w, b):
    m = jax.nn.silu(c) @ w + b
    return jnp.split(m[:, None, :], 6, axis=-1)


def ada_in(x, g, shift, scale):
    return rms_norm(x, g) * (1 + scale) + shift


def rope(x, pos):
    half = x.shape[-1] // 2
    inv_freq = ROPE_THETA ** (-jnp.arange(half, dtype=jnp.float32) / half)
    ang = pos.astype(jnp.float32)[:, None] * inv_freq[None, :]
    cos = jnp.cos(ang)[None, :, None, :]
    sin = jnp.sin(ang)[None, :, None, :]
    xf = x.astype(jnp.float32)
    x1, x2 = xf[..., :half], xf[..., half:]
    return jnp.concatenate([x1 * cos - x2 * sin, x2 * cos + x1 * sin], axis=-1).astype(x.dtype)


def gather_pages(cache, j, page_table):
    rows = cache[j, page_table]
    return rows.reshape((rows.shape[0], rows.shape[1] * rows.shape[2]) + rows.shape[3:])


def sweep_query_blocks(attend, n_q, n_past):
    outs = []
    for start in range(0, n_q, Q_BLOCK):
        stop = min(start + Q_BLOCK, n_q)
        outs.append(attend(start, stop, n_past + stop))
    return jnp.concatenate(outs, axis=1)


def sb_attend(q, q_pos, k, v):
    B, Tq, H, Dh = q.shape
    hkv = k.shape[2]
    qg = q.reshape(B, Tq, hkv, H // hkv, Dh)
    z = jnp.einsum('bqkgd,bskd->bkgqs', qg, k).astype(jnp.float32) * (Dh ** -0.5)
    past = jnp.arange(k.shape[1])[None, :] < q_pos[:, None]
    log_beta = jax.nn.log_sigmoid(z)
    log_stay = jnp.where(past, log_beta - z, 0.0)
    log_after = lax.cumsum(log_stay, axis=log_stay.ndim - 1, reverse=True) - log_stay
    w = jnp.where(past, jnp.exp(log_beta + log_after), 0.0)
    o = jnp.einsum('bkgqs,bskd->bqkgd', w.astype(v.dtype), v)
    return o.reshape(B, Tq, H, Dh)


def sb_mixer(h, pos, wq, wk, wv, wo, past_k, past_v):
    B, T, _ = h.shape
    q = (h @ wq).reshape(B, T, N_HEADS, HEAD_DIM)
    k = (h @ wk).reshape(B, T, SB_KV_HEADS, HEAD_DIM)
    v = (h @ wv).reshape(B, T, SB_KV_HEADS, HEAD_DIM)
    k_all = k if past_k is None else jnp.concatenate([past_k, k], axis=1)
    v_all = v if past_v is None else jnp.concatenate([past_v, v], axis=1)
    n_past = k_all.shape[1] - T
    o = sweep_query_blocks(
        lambda a, b, kend: sb_attend(q[:, a:b], pos[a:b], k_all[:, :kend], v_all[:, :kend]), T, n_past)
    return o.reshape(B, T, N_HEADS * HEAD_DIM) @ wo, k, v


def mla_attend(q_lat, q_pe, q_pos, ckv, kpe):
    s = (jnp.einsum('bqhc,bkc->bhqk', q_lat, ckv) + jnp.einsum('bqhr,bkr->bhqk', q_pe, kpe))
    s = s.astype(jnp.float32) * ((MLA_NOPE + MLA_ROPE) ** -0.5)
    mask = jnp.arange(ckv.shape[1])[None, :] <= q_pos[:, None]
    p = jax.nn.softmax(jnp.where(mask, s, NEG), axis=-1)
    return jnp.einsum('bhqk,bkc->bqhc', p.astype(ckv.dtype), ckv)


def mla_mixer(h, pos, wdq, q_norm, wuq, wdkv, kv_norm, wuk, wuv, wo, past_ckv, past_kpe):
    B, T, _ = h.shape
    q = (rms_norm(h @ wdq, q_norm) @ wuq).reshape(B, T, N_HEADS, MLA_NOPE + MLA_ROPE)
    q_lat = jnp.einsum('bthn,chn->bthc', q[..., :MLA_NOPE], wuk)
    q_pe = rope(q[..., MLA_NOPE:], pos)
    kv = h @ wdkv
    ckv = rms_norm(kv[..., :MLA_KV_LORA], kv_norm)
    kpe = rope(kv[..., None, MLA_KV_LORA:], pos)[:, :, 0]
    ckv_all = ckv if past_ckv is None else jnp.concatenate([past_ckv, ckv], axis=1)
    kpe_all = kpe if past_kpe is None else jnp.concatenate([past_kpe, kpe], axis=1)
    n_past = ckv_all.shape[1] - T
    o_lat = sweep_query_blocks(
        lambda a, b, kend: mla_attend(q_lat[:, a:b], q_pe[:, a:b], pos[a:b], ckv_all[:, :kend], kpe_all[:, :kend]),
        T, n_past)
    o = jnp.einsum('bthc,chv->bthv', o_lat, wuv).reshape(B, T, N_HEADS * MLA_V)
    return o @ wo, ckv, kpe


def moba_blocks(k, v):
    B, T, hkv, Dh = k.shape
    nb = -(-T // MOBA_BLOCK)
    pad = ((0, 0), (0, nb * MOBA_BLOCK - T), (0, 0), (0, 0))
    kb = jnp.pad(k, pad).reshape(B, nb, MOBA_BLOCK, hkv, Dh).transpose(0, 3, 1, 2, 4)
    vb = jnp.pad(v, pad).reshape(B, nb, MOBA_BLOCK, hkv, Dh).transpose(0, 3, 1, 2, 4)
    kmean = jnp.mean(kb.astype(jnp.float32), axis=3).astype(k.dtype)
    return kb, vb, kmean


def moba_attend(q, q_pos, kb, vb, kmean):
    B, Tq, H, Dh = q.shape
    hkv, nb = kb.shape[1], kb.shape[2]
    G = H // hkv
    qg = q.reshape(B, Tq, hkv, G, Dh)
    gate = jnp.einsum('bqkgd,bknd->bqkgn', qg, kmean).astype(jnp.float32).reshape(B, Tq, H, nb)
    own = q_pos // MOBA_BLOCK
    gate = jnp.where(jnp.arange(nb)[None, None, None, :] < own[None, :, None, None], gate, NEG)
    n_sel = min(MOBA_TOPK, nb)
    _, top = lax.top_k(gate, n_sel)
    own_idx = jnp.broadcast_to(own[None, :, None, None].astype(top.dtype), (B, Tq, H, 1))
    idx = jnp.concatenate([top, own_idx], axis=-1)
    b_ix = jnp.arange(B)[:, None, None, None]
    h_ix = (jnp.arange(H) // G)[None, None, :, None]
    s = jnp.einsum('bqhd,bqhrld->bqhrl', q, kb[b_ix, h_ix, idx]).astype(jnp.float32) * (Dh ** -0.5)
    key_pos = idx[..., None] * MOBA_BLOCK + jnp.arange(MOBA_BLOCK)
    slot = jnp.arange(n_sel + 1)[None, None, None, :, None]
    valid = jnp.where(slot == n_sel,
                      key_pos <= q_pos[None, :, None, None, None],
                      slot < own[None, :, None, None, None])
    s = jnp.where(valid, s, NEG)
    p = jax.nn.softmax(s.reshape(B, Tq, H, -1), axis=-1).reshape(s.shape)
    return jnp.einsum('bqhrl,bqhrld->bqhd', p.astype(vb.dtype), vb[b_ix, h_ix, idx])


def moba_mixer(h, pos, wq, wk, wv, wo, past_k, past_v):
    B, T, _ = h.shape
    q = rope((h @ wq).reshape(B, T, N_HEADS, HEAD_DIM), pos)
    k = rope((h @ wk).reshape(B, T, MOBA_KV_HEADS, HEAD_DIM), pos)
    v = (h @ wv).reshape(B, T, MOBA_KV_HEADS, HEAD_DIM)
    k_all = k if past_k is None else jnp.concatenate([past_k, k], axis=1)
    v_all = v if past_v is None else jnp.concatenate([past_v, v], axis=1)
    n_past = k_all.shape[1] - T
    kb, vb, kmean = moba_blocks(k_all, v_all)
    o = sweep_query_blocks(lambda a, b, kend: moba_attend(q[:, a:b], pos[a:b], kb, vb, kmean), T, n_past)
    return o.reshape(B, T, N_HEADS * HEAD_DIM) @ wo, k, v


def swiglu(x, wg, wu, wd):
    return (jax.nn.silu(x @ wg) * (x @ wu)) @ wd


def moe_swiglu(x, wr, br, wg, wu, wd):
    logits = (x @ wr + br).astype(jnp.float32)
    top_v, top_i = lax.top_k(logits, TOP_K)
    gates = jax.nn.softmax(top_v, axis=-1)
    combine = jnp.einsum('btk,btke->bte', gates, jax.nn.one_hot(top_i, N_EXPERTS, dtype=jnp.float32))
    y = jnp.zeros_like(x)
    for e in range(N_EXPERTS):
        y = y + combine[..., e:e + 1].astype(x.dtype) * swiglu(x, wg[e], wu[e], wd[e])
    return y


def setup_inputs(seed: int = 0) -> dict:
    key = jax.random.key(seed)
    ks = iter(jax.random.split(key, 48))

    def nrm(shape, scale):
        return jax.random.normal(next(ks), shape, jnp.float32) * scale

    def gain(shape):
        return 1.0 + nrm(shape, 0.05)

    n_pages = PAST_LEN // PAGE_SIZE
    n_used = DEC_BATCH * n_pages
    n_pool = n_used + max(1, n_used // 4)
    D = D_MODEL
    HD = N_HEADS * HEAD_DIM
    inp = {}
    inp['x_prompt'] = nrm((BATCH, SEQ, D), 1.0)
    inp['x_sample'] = nrm((DEC_BATCH, DEC_SEQ, D), 1.0)
    inp['cache_sb_k'] = nrm((N_SB, n_pool, PAGE_SIZE, SB_KV_HEADS, HEAD_DIM), 1.0)
    inp['cache_sb_v'] = nrm((N_SB, n_pool, PAGE_SIZE, SB_KV_HEADS, HEAD_DIM), 1.0)
    inp['cache_mla_ckv'] = nrm((N_MLA, n_pool, PAGE_SIZE, MLA_KV_LORA), 1.0)
    inp['cache_mla_kpe'] = nrm((N_MLA, n_pool, PAGE_SIZE, MLA_ROPE), 1.0)
    inp['cache_moba_k'] = nrm((N_MOBA, n_pool, PAGE_SIZE, MOBA_KV_HEADS, HEAD_DIM), 1.0)
    inp['cache_moba_v'] = nrm((N_MOBA, n_pool, PAGE_SIZE, MOBA_KV_HEADS, HEAD_DIM), 1.0)
    inp['page_table'] = jax.random.permutation(next(ks), n_pool)[:n_used].reshape(DEC_BATCH, n_pages).astype(jnp.int32)
    inp['c_prompt'] = nrm((BATCH, D), 1.0)
    inp['c_sample'] = nrm((DEC_BATCH, D), 1.0)
    inp['mod_w'] = nrm((DEPTH, D, 6 * D), 0.5 * D ** -0.5)
    inp['mod_b'] = nrm((DEPTH, 6 * D), 0.02)
    inp['norm_mix'] = gain((DEPTH, D))
    inp['norm_ffn'] = gain((DEPTH, D))
    inp['final_norm'] = gain((D,))
    inp['sb_wq'] = nrm((N_SB, D, HD), D ** -0.5)
    inp['sb_wk'] = nrm((N_SB, D, SB_KV_HEADS * HEAD_DIM), D ** -0.5)
    inp['sb_wv'] = nrm((N_SB, D, SB_KV_HEADS * HEAD_DIM), D ** -0.5)
    inp['sb_wo'] = nrm((N_SB, HD, D), HD ** -0.5)
    inp['mla_wdq'] = nrm((N_MLA, D, MLA_Q_LORA), D ** -0.5)
    inp['mla_q_norm'] = gain((N_MLA, MLA_Q_LORA))
    inp['mla_wuq'] = nrm((N_MLA, MLA_Q_LORA, N_HEADS * (MLA_NOPE + MLA_ROPE)), MLA_Q_LORA ** -0.5)
    inp['mla_wdkv'] = nrm((N_MLA, D, MLA_KV_LORA + MLA_ROPE), D ** -0.5)
    inp['mla_kv_norm'] = gain((N_MLA, MLA_KV_LORA))
    inp['mla_wuk'] = nrm((N_MLA, MLA_KV_LORA, N_HEADS, MLA_NOPE), MLA_KV_LORA ** -0.5)
    inp['mla_wuv'] = nrm((N_MLA, MLA_KV_LORA, N_HEADS, MLA_V), MLA_KV_LORA ** -0.5)
    inp['mla_wo'] = nrm((N_MLA, N_HEADS * MLA_V, D), (N_HEADS * MLA_V) ** -0.5)
    inp['moba_wq'] = nrm((N_MOBA, D, HD), D ** -0.5)
    inp['moba_wk'] = nrm((N_MOBA, D, MOBA_KV_HEADS * HEAD_DIM), D ** -0.5)
    inp['moba_wv'] = nrm((N_MOBA, D, MOBA_KV_HEADS * HEAD_DIM), D ** -0.5)
    inp['moba_wo'] = nrm((N_MOBA, HD, D), HD ** -0.5)
    inp['ffn_wg'] = nrm((N_DENSE, D, D_FF), D ** -0.5)
    inp['ffn_wu'] = nrm((N_DENSE, D, D_FF), D ** -0.5)
    inp['ffn_wd'] = nrm((N_DENSE, D_FF, D), D_FF ** -0.5)
    inp['moe_wr'] = nrm((N_MOE, D, N_EXPERTS), D ** -0.5)
    inp['moe_br'] = nrm((N_MOE, N_EXPERTS), 0.01)
    inp['moe_wg'] = nrm((N_MOE, N_EXPERTS, D, D_FF_EXPERT), D ** -0.5)
    inp['moe_wu'] = nrm((N_MOE, N_EXPERTS, D, D_FF_EXPERT), D ** -0.5)
    inp['moe_wd'] = nrm((N_MOE, N_EXPERTS, D_FF_EXPERT, D), D_FF_EXPERT ** -0.5)
    return inp


def reference(x_prompt, x_sample, cache_sb_k, cache_sb_v, cache_mla_ckv, cache_mla_kpe, cache_moba_k,
              cache_moba_v, page_table, c_prompt, c_sample, mod_w, mod_b, norm_mix, norm_ffn, final_norm,
              sb_wq, sb_wk, sb_wv, sb_wo, mla_wdq, mla_q_norm, mla_wuq, mla_wdkv, mla_kv_norm, mla_wuk,
              mla_wuv, mla_wo, moba_wq, moba_wk, moba_wv, moba_wo, ffn_wg, ffn_wu, ffn_wd, moe_wr, moe_br,
              moe_wg, moe_wu, moe_wd):
    past_len = page_table.shape[1] * cache_sb_k.shape[2]
    pos_p = jnp.arange(x_prompt.shape[1], dtype=jnp.int32)
    pos_s = past_len + jnp.arange(x_sample.shape[1], dtype=jnp.int32)
    sb_k_p, sb_v_p, sb_k_s, sb_v_s = [], [], [], []
    mla_c_p, mla_r_p, mla_c_s, mla_r_s = [], [], [], []
    mb_k_p, mb_v_p, mb_k_s, mb_v_s = [], [], [], []
    xp, xs = x_prompt, x_sample
    for i in range(DEPTH):
        sha_p, sca_p, ga_p, shf_p, scf_p, gf_p = modulation(c_prompt, mod_w[i], mod_b[i])
        sha_s, sca_s, ga_s, shf_s, scf_s, gf_s = modulation(c_sample, mod_w[i], mod_b[i])
        hp = ada_in(xp, norm_mix[i], sha_p, sca_p)
        hs = ada_in(xs, norm_mix[i], sha_s, sca_s)
        kind, j = i % N_MIXERS, i // N_MIXERS
        if kind == 0:
            mp, kp, vp = sb_mixer(hp, pos_p, sb_wq[j], sb_wk[j], sb_wv[j], sb_wo[j], None, None)
            ms, ks_, vs_ = sb_mixer(hs, pos_s, sb_wq[j], sb_wk[j], sb_wv[j], sb_wo[j],
                                    gather_pages(cache_sb_k, j, page_table), gather_pages(cache_sb_v, j, page_table))
            sb_k_p.append(kp); sb_v_p.append(vp); sb_k_s.append(ks_); sb_v_s.append(vs_)
        elif kind == 1:
            mp, cp, rp = mla_mixer(hp, pos_p, mla_wdq[j], mla_q_norm[j], mla_wuq[j], mla_wdkv[j], mla_kv_norm[j],
                                   mla_wuk[j], mla_wuv[j], mla_wo[j], None, None)
            ms, cs, rs = mla_mixer(hs, pos_s, mla_wdq[j], mla_q_norm[j], mla_wuq[j], mla_wdkv[j], mla_kv_norm[j],
                                   mla_wuk[j], mla_wuv[j], mla_wo[j],
                                   gather_pages(cache_mla_ckv, j, page_table), gather_pages(cache_mla_kpe, j, page_table))
            mla_c_p.append(cp); mla_r_p.append(rp); mla_c_s.append(cs); mla_r_s.append(rs)
        else:
            mp, kp, vp = moba_mixer(hp, pos_p, moba_wq[j], moba_wk[j], moba_wv[j], moba_wo[j], None, None)
            ms, ks_, vs_ = moba_mixer(hs, pos_s, moba_wq[j], moba_wk[j], moba_wv[j], moba_wo[j],
                                      gather_pages(cache_moba_k, j, page_table), gather_pages(cache_moba_v, j, page_table))
            mb_k_p.append(kp); mb_v_p.append(vp); mb_k_s.append(ks_); mb_v_s.append(vs_)
        xp = xp + ga_p * mp
        xs = xs + ga_s * ms
        hp = ada_in(xp, norm_ffn[i], shf_p, scf_p)
        hs = ada_in(xs, norm_ffn[i], shf_s, scf_s)
        f = i // 2
        if i % 2 == 0:
            fp = swiglu(hp, ffn_wg[f], ffn_wu[f], ffn_wd[f])
            fs = swiglu(hs, ffn_wg[f], ffn_wu[f], ffn_wd[f])
        else:
            fp = moe_swiglu(hp, moe_wr[f], moe_br[f], moe_wg[f], moe_wu[f], moe_wd[f])
            fs = moe_swiglu(hs, moe_wr[f], moe_br[f], moe_wg[f], moe_wu[f], moe_wd[f])
        xp = xp + gf_p * fp
        xs = xs + gf_s * fs
    y_prompt = rms_norm(xp, final_norm)
    y_sample = rms_norm(xs, final_norm)
    return (y_prompt, y_sample,
            jnp.stack(sb_k_p), jnp.stack(sb_v_p), jnp.stack(sb_k_s), jnp.stack(sb_v_s),
            jnp.stack(mla_c_p), jnp.stack(mla_r_p), jnp.stack(mla_c_s), jnp.stack(mla_r_s),
            jnp.stack(mb_k_p), jnp.stack(mb_v_p), jnp.stack(mb_k_s), jnp.stack(mb_v_s))
```

```python
import functools

import jax
import jax.numpy as jnp
from jax import lax
from jax.experimental import pallas as pl
from jax.experimental.pallas import tpu as pltpu

F32 = jnp.float32
BF16 = jnp.bfloat16

LANES = 128
SUBLANES = 8
HEAD_DIM = 128
EPS = 1e-6
NEG = -1e30
ROPE_THETA = 10000.0
MOBA_BLOCK = 256
MOBA_TOPK = 3
N_MIXERS = 3
VMEM_LIMIT = 56 << 20


def _params(*sem):
    return pltpu.CompilerParams(dimension_semantics=sem, vmem_limit_bytes=VMEM_LIMIT)


def _nt(a, b):
    return lax.dot_general(a, b, (((1,), (1,)), ((), ())), preferred_element_type=F32)


def _nn(a, b):
    return jnp.dot(a, b, preferred_element_type=F32)


def _split(x):
    hi = x.astype(BF16)
    lo = (x - hi.astype(F32)).astype(BF16)
    return hi, lo


def _nt3(a, b):
    ah, al = _split(a)
    bh, bl = _split(b)
    return _nt(ah, bh) + _nt(ah, bl) + _nt(al, bh)


def _softplus(z):
    return jnp.maximum(z, 0.0) + jnp.log(1.0 + jnp.exp(-jnp.abs(z)))


def _silu(g):
    return g * (1.0 / (1.0 + jnp.exp(-g)))


def _rms(x, g):
    return x * lax.rsqrt(jnp.mean(x * x, axis=-1, keepdims=True) + EPS) * g


def _row_tiling(G, R, tm):
    if R >= tm:
        assert R % tm == 0
        return 1, tm, R // tm
    assert R % SUBLANES == 0
    gb = max(min(tm // R, G), 1)
    while G % gb:
        gb -= 1
    return gb, R, 1


def _log2(n):
    assert n > 0 and n & (n - 1) == 0, n
    return n.bit_length() - 1


def _imod(x, n):
    _log2(n)
    return x & (n - 1)


def _idiv(x, n):
    return x >> _log2(n)


def _mod_spec(mod, j, gb, rb, nrb, width, col=None):
    _, gm, rm, _ = mod.shape
    assert gm == 1 or rm == 1
    block = (None, gb if gm > 1 else 1, rb if rm > 1 else 1, width)

    def imap(*ids):
        i = ids[0]
        return (j, i // nrb if gm > 1 else 0, i % nrb if rm > 1 else 0, 0 if col is None else ids[col])

    return pl.BlockSpec(block, imap)


def _fit(n, want):
    t = min(n, want)
    while n % t:
        t -= LANES if t > LANES else 1
    return t


def _rope_tables(pos, dim):
    half = dim // 2
    inv_freq = ROPE_THETA ** (-jnp.arange(half, dtype=F32) / half)
    ang = pos.astype(F32)[:, None] * inv_freq[None, :]
    cos, sin = jnp.cos(ang), jnp.sin(ang)
    z = jnp.zeros((pos.shape[0], LANES - dim), F32)
    zh = jnp.zeros_like(cos)
    c = jnp.concatenate([cos, cos, z], axis=1)
    s1 = jnp.concatenate([-sin, zh, z], axis=1)
    s2 = jnp.concatenate([zh, sin, z], axis=1)
    return c, s1, s2


def _rope128(a, c, s):
    return a * c + pltpu.roll(a, HEAD_DIM // 2, axis=1) * s


def _rope64(a, c, s1, s2, half):
    return a * c + pltpu.roll(a, LANES - half, axis=1) * s1 + pltpu.roll(a, half, axis=1) * s2


def _mod_body(c_ref, w_ref, b_ref, o_ref):
    a = _silu(c_ref[...]).astype(BF16)
    o_ref[...] = _nn(a, w_ref[...].astype(BF16)) + b_ref[...]


def modulation_all(c_all, mod_w, mod_b):
    L, D, N = mod_w.shape
    Bp = c_all.shape[0]
    tn = _fit(N, 1024)
    return pl.pallas_call(
        _mod_body,
        grid=(L, N // tn),
        in_specs=[pl.BlockSpec((Bp, D), lambda l, n: (0, 0)),
                  pl.BlockSpec((None, D, tn), lambda l, n: (l, 0, n)),
                  pl.BlockSpec((None, 1, tn), lambda l, n: (l, 0, n))],
        out_specs=pl.BlockSpec((None, Bp, tn), lambda l, n: (l, 0, n)),
        out_shape=jax.ShapeDtypeStruct((L, Bp, N), F32),
        compiler_params=_params("parallel", "parallel"),
        name="modulation",
    )(c_all, mod_w, mod_b.reshape(L, 1, N))


def _ada_h(x_ref, sh_ref, sc_ref, g_ref):
    x = x_ref[...]
    h = _rms(x, g_ref[...]) * (1.0 + sc_ref[...]) + sh_ref[...]
    return h.reshape(h.shape[0] * h.shape[1], h.shape[2])


def _ada_linear_body(x_ref, sh_ref, sc_ref, g_ref, w_ref, *rest, epilogue, n_extra):
    extra, o_ref, h_scr = rest[:n_extra], rest[n_extra], rest[n_extra + 1]
    n = pl.program_id(1)

    @pl.when(n == 0)
    def _():
        h_scr[...] = _ada_h(x_ref, sh_ref, sc_ref, g_ref).astype(BF16)

    acc = _nn(h_scr[...], w_ref[...].astype(BF16))
    epilogue(acc, n, extra, o_ref)


def _epi_plain(acc, n, extra, o_ref):
    o_ref[...] = acc.astype(o_ref.dtype)


def _epi_rope128(acc, n, extra, o_ref, *, n_rope_tiles):
    c_ref, s_ref = extra

    @pl.when(n < n_rope_tiles)
    def _():
        c, s = c_ref[...], s_ref[...]
        for j in range(acc.shape[1] // LANES):
            sl = slice(j * LANES, (j + 1) * LANES)
            o_ref[:, sl] = _rope128(acc[:, sl], c, s).astype(o_ref.dtype)

    @pl.when(n >= n_rope_tiles)
    def _():
        o_ref[...] = acc.astype(o_ref.dtype)


def _epi_mla_down(acc, n, extra, o_ref, *, q_lora, kv_lora, half):
    qn_ref, kn_ref, c_ref, s1_ref, s2_ref = extra
    o_ref[:, 0:q_lora] = _rms(acc[:, 0:q_lora], qn_ref[...])
    o_ref[:, q_lora:q_lora + kv_lora] = _rms(acc[:, q_lora:q_lora + kv_lora], kn_ref[...])
    r0 = q_lora + kv_lora
    o_ref[:, r0:r0 + LANES] = _rope64(acc[:, r0:r0 + LANES], c_ref[...], s1_ref[...], s2_ref[...], half)


def ada_linear(x3, mod, j_shift, j_scale, gain, w, *, tm, tn, epilogue=_epi_plain, extra=(), extra_specs=(),
               out_dtype=F32, name):
    G, R, K = x3.shape
    N = w.shape[1]
    gb, rb, nrb = _row_tiling(G, R, tm)
    tm = gb * rb
    tn = _fit(N, tn)
    body = functools.partial(_ada_linear_body, epilogue=epilogue, n_extra=len(extra))
    mspec = lambda j: _mod_spec(mod, j, gb, rb, nrb, K)
    return pl.pallas_call(
        body,
        grid=(G * R // tm, N // tn),
        in_specs=[pl.BlockSpec((gb, rb, K), lambda i, n: (i // nrb, i % nrb, 0)),
                  mspec(j_shift), mspec(j_scale),
                  pl.BlockSpec((1, K), lambda i, n: (0, 0)),
                  pl.BlockSpec((K, tn), lambda i, n: (0, n))] + list(extra_specs),
        out_specs=pl.BlockSpec((tm, tn), lambda i, n: (i, n)),
        out_shape=jax.ShapeDtypeStruct((G * R, N), out_dtype),
        scratch_shapes=[pltpu.VMEM((tm, K), BF16)],
        compiler_params=_params("parallel", "arbitrary"),
        name=name,
    )(x3, mod, mod, gain.reshape(1, K), w, *extra)


def _row_tables(tabs, G, R, tm):
    gb, rb, _ = _row_tiling(G, R, tm)
    tm = gb * rb
    n = tabs[0].shape[0]
    if n < tm:
        assert tm % n == 0
        tabs = tuple(jnp.tile(t, (tm // n, 1)) for t in tabs)
        n = tm
    assert n % tm == 0
    return tabs, [pl.BlockSpec((tm, LANES), lambda i, c: (i % (n // tm), 0)) for _ in tabs]


def _const_spec(width):
    return pl.BlockSpec((1, width), lambda i, n: (0, 0))


def _linear_body(x_ref, w_ref, *rest, epilogue, n_extra, transpose_w):
    extra, o_ref = rest[:n_extra], rest[n_extra]
    x = x_ref[...].astype(BF16)
    w = w_ref[...].astype(BF16)
    acc = _nt(x, w) if transpose_w else _nn(x, w)
    epilogue(acc, pl.program_id(1), extra, o_ref)


def _epi_q_up(acc, n, extra, o_ref, *, half):
    c_ref, s1_ref, s2_ref = extra
    for j in range(acc.shape[1] // LANES):
        sl = slice(j * LANES, (j + 1) * LANES)
        a = acc[:, sl]
        if j % 2:
            a = _rope64(a, c_ref[...], s1_ref[...], s2_ref[...], half)
        o_ref[:, sl] = a.astype(o_ref.dtype)


def linear(x, w, *, x_cols, tm, tn, out_dtype, epilogue=_epi_plain, extra=(), extra_specs=(), name):
    M = x.shape[0]
    kw, kb = x_cols
    N = w.shape[1]
    tm, tn = _fit(M, tm), _fit(N, tn)
    body = functools.partial(_linear_body, epilogue=epilogue, n_extra=len(extra), transpose_w=False)
    return pl.pallas_call(
        body,
        grid=(M // tm, N // tn),
        in_specs=[pl.BlockSpec((tm, kw), lambda i, n: (i, kb)),
                  pl.BlockSpec((kw, tn), lambda i, n: (0, n))] + list(extra_specs),
        out_specs=pl.BlockSpec((tm, tn), lambda i, n: (i, n)),
        out_shape=jax.ShapeDtypeStruct((M, N), out_dtype),
        compiler_params=_params("parallel", "arbitrary"),
        name=name,
    )(x, w, *extra)


def head_linear(x, w, *, n_heads, x_width, x_stride, x_off, w_width, transpose_w, out_width, tm, name):
    M = x.shape[0]
    tm = _fit(M, tm)
    body = functools.partial(_linear_body, epilogue=_epi_plain, n_extra=0, transpose_w=transpose_w)
    return pl.pallas_call(
        body,
        grid=(M // tm, n_heads),
        in_specs=[pl.BlockSpec((tm, x_width), lambda i, h: (i, h * x_stride + x_off)),
                  pl.BlockSpec((w.shape[0], w_width), lambda i, h: (0, h))],
        out_specs=pl.BlockSpec((tm, out_width), lambda i, h: (i, h)),
        out_shape=jax.ShapeDtypeStruct((M, n_heads * out_width), BF16),
        compiler_params=_params("parallel", "arbitrary"),
        name=name,
    )(x, w)


def _linear_res_body(a_ref, w_ref, x_ref, gate_ref, o_ref):
    acc = _nn(a_ref[...].astype(BF16), w_ref[...].astype(BF16))
    o_ref[...] = x_ref[...] + gate_ref[...] * acc.reshape(x_ref.shape)


def linear_res(a, w, x3, mod, j_gate, *, tm, tn, name):
    G, R, N = x3.shape
    K = a.shape[1]
    gb, rb, nrb = _row_tiling(G, R, tm)
    tm = gb * rb
    tn = _fit(N, tn)
    return pl.pallas_call(
        _linear_res_body,
        grid=(G * R // tm, N // tn),
        in_specs=[pl.BlockSpec((tm, K), lambda i, n: (i, 0)),
                  pl.BlockSpec((K, tn), lambda i, n: (0, n)),
                  pl.BlockSpec((gb, rb, tn), lambda i, n: (i // nrb, i % nrb, n)),
                  _mod_spec(mod, j_gate, gb, rb, nrb, tn, col=1)],
        out_specs=pl.BlockSpec((gb, rb, tn), lambda i, n: (i // nrb, i % nrb, n)),
        out_shape=jax.ShapeDtypeStruct(x3.shape, F32),
        compiler_params=_params("parallel", "arbitrary"),
        name=name,
    )(a, w, x3, mod)


def _ffn_body(x_ref, sh_ref, sc_ref, gate_ref, g_ref, wg_ref, wu_ref, wd_ref, o_ref, h_scr, acc_scr):
    f = pl.program_id(1)

    @pl.when(f == 0)
    def _():
        h_scr[...] = _ada_h(x_ref, sh_ref, sc_ref, g_ref).astype(BF16)
        acc_scr[...] = jnp.zeros_like(acc_scr)

    h = h_scr[...]
    a = _silu(_nn(h, wg_ref[...].astype(BF16))) * _nn(h, wu_ref[...].astype(BF16))
    acc_scr[...] += _nn(a.astype(BF16), wd_ref[...].astype(BF16))

    @pl.when(f == pl.num_programs(1) - 1)
    def _():
        o_ref[...] = x_ref[...] + gate_ref[...] * acc_scr[...].reshape(x_ref.shape)


def ffn_dense(x3, mod, gain, wg, wu, wd, *, tm, tf):
    G, R, D = x3.shape
    Fd = wg.shape[1]
    gb, rb, nrb = _row_tiling(G, R, tm)
    tm = gb * rb
    tf = _fit(Fd, tf)
    mspec = lambda j: _mod_spec(mod, j, gb, rb, nrb, D)
    xspec = pl.BlockSpec((gb, rb, D), lambda i, f: (i // nrb, i % nrb, 0))
    return pl.pallas_call(
        _ffn_body,
        grid=(G * R // tm, Fd // tf),
        in_specs=[xspec, mspec(3), mspec(4), mspec(5),
                  pl.BlockSpec((1, D), lambda i, f: (0, 0)),
                  pl.BlockSpec((D, tf), lambda i, f: (0, f)),
                  pl.BlockSpec((D, tf), lambda i, f: (0, f)),
                  pl.BlockSpec((tf, D), lambda i, f: (f, 0))],
        out_specs=xspec,
        out_shape=jax.ShapeDtypeStruct(x3.shape, F32),
        scratch_shapes=[pltpu.VMEM((tm, D), BF16), pltpu.VMEM((tm, D), F32)],
        compiler_params=_params("parallel", "arbitrary"),
        name="ffn_dense",
    )(x3, mod, mod, mod, gain.reshape(1, D), wg, wu, wd)


def _top2_combine(logits, lane):
    m1 = jnp.max(logits, axis=1, keepdims=True)
    i1 = jnp.min(jnp.where(logits == m1, lane, float(LANES)), axis=1, keepdims=True)
    rest = jnp.where(lane == i1, -jnp.inf, logits)
    m2 = jnp.max(rest, axis=1, keepdims=True)
    i2 = jnp.min(jnp.where(rest == m2, lane, float(LANES)), axis=1, keepdims=True)
    e = jnp.exp(m2 - m1)
    den = 1.0 + e
    return jnp.where(lane == i1, 1.0 / den, 0.0) + jnp.where(lane == i2, e / den, 0.0)


def _moe_body(x_ref, sh_ref, sc_ref, gate_ref, g_ref, wr_ref, br_ref, wg_ref, wu_ref, wd_ref, o_ref,
              h_scr, acc_scr, comb_scr):
    e, f = pl.program_id(1), pl.program_id(2)
    first = jnp.logical_and(e == 0, f == 0)
    last = jnp.logical_and(e == pl.num_programs(1) - 1, f == pl.num_programs(2) - 1)

    @pl.when(first)
    def _():
        h = _ada_h(x_ref, sh_ref, sc_ref, g_ref)
        h_scr[...] = h.astype(BF16)
        acc_scr[...] = jnp.zeros_like(acc_scr)
        hh, hl = _split(h)
        wh, wl = _split(wr_ref[...])
        logits = _nn(hh, wh) + _nn(hh, wl) + _nn(hl, wh) + br_ref[...]
        lane = lax.broadcasted_iota(jnp.int32, logits.shape, 1).astype(F32)
        comb_scr[...] = _top2_combine(logits, lane)

    comb = comb_scr[...]
    lane = lax.broadcasted_iota(jnp.int32, comb.shape, 1)
    ce = jnp.sum(jnp.where(lane == e, comb, 0.0), axis=1, keepdims=True)
    h = h_scr[...]
    a = _silu(_nn(h, wg_ref[...].astype(BF16))) * _nn(h, wu_ref[...].astype(BF16))
    acc_scr[...] += _nn((a * ce).astype(BF16), wd_ref[...].astype(BF16))

    @pl.when(last)
    def _():
        o_ref[...] = x_ref[...] + gate_ref[...] * acc_scr[...].reshape(x_ref.shape)


def moe_dense(x3, mod, gain, wr, br, wg, wu, wd, *, tm, tf):
    G, R, D = x3.shape
    E, _, Fe = wg.shape
    gb, rb, nrb = _row_tiling(G, R, tm)
    tm = gb * rb
    tf = _fit(Fe, tf)
    wr_pad = jnp.zeros((D, LANES), F32).at[:, :E].set(wr)
    br_pad = jnp.full((1, LANES), NEG, F32).at[0, :E].set(br)
    mspec = lambda j: _mod_spec(mod, j, gb, rb, nrb, D)
    xspec = pl.BlockSpec((gb, rb, D), lambda i, e, f: (i // nrb, i % nrb, 0))
    return pl.pallas_call(
        _moe_body,
        grid=(G * R // tm, E, Fe // tf),
        in_specs=[xspec, mspec(3), mspec(4), mspec(5),
                  pl.BlockSpec((1, D), lambda i, e, f: (0, 0)),
                  pl.BlockSpec((D, LANES), lambda i, e, f: (0, 0)),
                  pl.BlockSpec((1, LANES), lambda i, e, f: (0, 0)),
                  pl.BlockSpec((None, D, tf), lambda i, e, f: (e, 0, f)),
                  pl.BlockSpec((None, D, tf), lambda i, e, f: (e, 0, f)),
                  pl.BlockSpec((None, tf, D), lambda i, e, f: (e, f, 0))],
        out_specs=xspec,
        out_shape=jax.ShapeDtypeStruct(x3.shape, F32),
        scratch_shapes=[pltpu.VMEM((tm, D), BF16), pltpu.VMEM((tm, D), F32), pltpu.VMEM((tm, LANES), F32)],
        compiler_params=_params("parallel", "arbitrary", "arbitrary"),
        name="moe_dense",
    )(x3, mod, mod, mod, gain.reshape(1, D), wr_pad, br_pad, wg, wu, wd)


def _final_norm_body(x_ref, g_ref, o_ref):
    o_ref[...] = _rms(x_ref[...], g_ref[...])


def rms_final(x3, gain, *, tm):
    G, R, D = x3.shape
    gb, rb, nrb = _row_tiling(G, R, tm)
    spec = pl.BlockSpec((gb, rb, D), lambda i: (i // nrb, i % nrb, 0))
    return pl.pallas_call(
        _final_norm_body,
        grid=(G * R // (gb * rb),),
        in_specs=[spec, pl.BlockSpec((1, D), lambda i: (0, 0))],
        out_specs=spec,
        out_shape=jax.ShapeDtypeStruct(x3.shape, F32),
        compiler_params=_params("parallel"),
        name="final_norm",
    )(x3, gain.reshape(1, D))


def _cumsum_matrix(tk):
    j = lax.broadcasted_iota(jnp.int32, (2 * tk, tk), 0)
    s = lax.broadcasted_iota(jnp.int32, (2 * tk, tk), 1)
    jj = jnp.where(j >= tk, j - tk, j)
    return jnp.where(jj > s, 1.0, 0.0).astype(BF16)


def _sb_tile(z, past, carry, u2):
    sp = _softplus(z)
    log_beta = z - sp
    log_stay = -sp
    if past is not None:
        log_stay = jnp.where(past, log_stay, 0.0)
    hi, lo = _split(log_stay)
    local = _nn(jnp.concatenate([hi, lo], axis=1), u2)
    w = jnp.exp(log_beta + local + carry)
    if past is not None:
        w = jnp.where(past, w, 0.0)
    return w, jnp.sum(log_stay, axis=1, keepdims=True)


def _sb_prompt_body(q_ref, k_ref, v_ref, o_ref, acc_scr, carry_scr, *, tq, n_group, scale):
    qt = pl.program_id(2)
    qs = jnp.concatenate([q_ref[:, g * HEAD_DIM:(g + 1) * HEAD_DIM] for g in range(n_group)], axis=0).astype(BF16)
    u2 = _cumsum_matrix(tq)
    rows = n_group * tq

    def tile(kt, masked):
        start = pl.multiple_of(kt * tq, tq)
        k = k_ref[pl.ds(start, tq), :].astype(BF16)
        v = v_ref[pl.ds(start, tq), :].astype(BF16)
        z = _nt(qs, k) * scale
        past = None
        if masked:
            r = lax.broadcasted_iota(jnp.int32, (rows, tq), 0)
            c = lax.broadcasted_iota(jnp.int32, (rows, tq), 1)
            past = c < _imod(r, tq)
        w, tot = _sb_tile(z, past, carry_scr[...], u2)
        acc_scr[...] += _nn(w.astype(BF16), v)
        carry_scr[...] += tot

    acc_scr[...] = jnp.zeros_like(acc_scr)
    carry_scr[...] = jnp.zeros_like(carry_scr)
    tile(qt, True)

    def body(j, _):
        tile(qt - 1 - j, False)
        return 0

    lax.fori_loop(0, qt, body, 0)
    acc = acc_scr[...]
    for g in range(n_group):
        o_ref[:, g * HEAD_DIM:(g + 1) * HEAD_DIM] = acc[g * tq:(g + 1) * tq, :].astype(o_ref.dtype)


def sb_prompt(qkv, *, B, T, n_heads, n_kv, tq=128):
    n_group = n_heads // n_kv
    nqt = T // tq
    gw = n_group * HEAD_DIM
    body = functools.partial(_sb_prompt_body, tq=tq, n_group=n_group, scale=HEAD_DIM ** -0.5)
    return pl.pallas_call(
        body,
        grid=(B, n_kv, nqt),
        in_specs=[pl.BlockSpec((tq, gw), lambda b, kh, qt: (b * nqt + qt, kh)),
                  pl.BlockSpec((T, HEAD_DIM), lambda b, kh, qt: (b, n_heads + kh)),
                  pl.BlockSpec((T, HEAD_DIM), lambda b, kh, qt: (b, n_heads + n_kv + kh))],
        out_specs=pl.BlockSpec((tq, gw), lambda b, kh, qt: (b * nqt + qt, kh)),
        out_shape=jax.ShapeDtypeStruct((B * T, n_heads * HEAD_DIM), BF16),
        scratch_shapes=[pltpu.VMEM((n_group * tq, HEAD_DIM), F32), pltpu.VMEM((n_group * tq, 1), F32)],
        compiler_params=_params("parallel", "parallel", "arbitrary"),
        name="sb_prompt",
    )(qkv, qkv, qkv)


def _page_specs(layer, n_pages, P, page_shape, reverse):
    def spec(i):
        def imap(b, c, pt):
            p = c * P + i
            if reverse:
                p = n_pages - 1 - p
            return (layer, pt[b, p]) + (0,) * len(page_shape)
        return pl.BlockSpec((None, None) + tuple(page_shape), imap)
    return [spec(i) for i in range(P)]


def _page_heads(ref, n_kv):
    page = ref.shape[0] // n_kv
    return [ref[pl.ds(h, page, stride=n_kv), :].astype(BF16) for h in range(n_kv)]


def _lane_heads(x, n_kv):
    return [x[:, h * HEAD_DIM:(h + 1) * HEAD_DIM].astype(BF16) for h in range(n_kv)]


def _sb_sample_body(pt_ref, q_ref, kn_ref, vn_ref, *rest, P, n_kv, t_new, scale):
    k_refs, v_refs = rest[:P], rest[P:2 * P]
    o_ref, acc_scr, carry_scr = rest[2 * P], rest[2 * P + 1], rest[2 * P + 2]
    c = pl.program_id(1)
    q = q_ref[...].astype(BF16)
    rows = q.shape[0]
    rk = rows // n_kv
    u2 = _cumsum_matrix(LANES)

    def tile(ks, vs, masked):
        z = jnp.concatenate([_nt(q[h * rk:(h + 1) * rk], ks[h]) for h in range(n_kv)], axis=0) * scale
        past = None
        if masked:
            r = lax.broadcasted_iota(jnp.int32, z.shape, 0)
            col = lax.broadcasted_iota(jnp.int32, z.shape, 1)
            past = col < _imod(r, t_new)
        w, tot = _sb_tile(z, past, carry_scr[...], u2)
        wb = w.astype(BF16)
        for h in range(n_kv):
            acc_scr[h * rk:(h + 1) * rk, :] += _nn(wb[h * rk:(h + 1) * rk], vs[h])
        carry_scr[...] += tot

    @pl.when(c == 0)
    def _():
        acc_scr[...] = jnp.zeros_like(acc_scr)
        carry_scr[...] = jnp.zeros_like(carry_scr)
        tile(_lane_heads(kn_ref[...], n_kv), _lane_heads(vn_ref[...], n_kv), True)

    for i in range(P):
        tile(_page_heads(k_refs[i], n_kv), _page_heads(v_refs[i], n_kv), False)

    @pl.when(c == pl.num_programs(1) - 1)
    def _():
        o_ref[...] = acc_scr[...].astype(o_ref.dtype)


def sb_sample(q, k_new, v_new, cache_k, cache_v, layer, page_table, *, n_kv, t_new, P):
    B, rows, _ = q.shape
    n_pages = page_table.shape[1]
    P = _fit(n_pages, P)
    kvw = n_kv * HEAD_DIM
    page_shape = cache_k.shape[2:]
    body = functools.partial(_sb_sample_body, P=P, n_kv=n_kv, t_new=t_new, scale=HEAD_DIM ** -0.5)
    bspec = lambda w: pl.BlockSpec((None, rows if w == HEAD_DIM else LANES, w), lambda b, c, pt: (b, 0, 0))
    grid_spec = pltpu.PrefetchScalarGridSpec(
        num_scalar_prefetch=1,
        grid=(B, n_pages // P),
        in_specs=[bspec(HEAD_DIM), bspec(kvw), bspec(kvw)]
        + _page_specs(layer, n_pages, P, page_shape, True) + _page_specs(layer, n_pages, P, page_shape, True),
        out_specs=pl.BlockSpec((None, rows, HEAD_DIM), lambda b, c, pt: (b, 0, 0)),
        scratch_shapes=[pltpu.VMEM((rows, HEAD_DIM), F32), pltpu.VMEM((rows, 1), F32)],
    )
    return pl.pallas_call(
        body,
        grid_spec=grid_spec,
        out_shape=jax.ShapeDtypeStruct((B, rows, HEAD_DIM), BF16),
        compiler_params=_params("parallel", "arbitrary"),
        name="sb_sample",
    )(page_table, q, k_new, v_new, *([cache_k] * P), *([cache_v] * P))


def _softmax_step(s_list, m_scr, l_scr):
    m_old = m_scr[...]
    m_new = m_old
    for s in s_list:
        m_new = jnp.maximum(m_new, jnp.max(s, axis=1, keepdims=True))
    alpha = jnp.exp(m_old - m_new)
    p_list = [jnp.exp(s - m_new) for s in s_list]
    l = alpha * l_scr[...]
    for p in p_list:
        l = l + jnp.sum(p, axis=1, keepdims=True)
    m_scr[...] = m_new
    l_scr[...] = l
    return alpha, p_list


def _softmax_init(m_scr, l_scr, acc_scr):
    m_scr[...] = jnp.full(m_scr.shape, NEG, F32)
    l_scr[...] = jnp.zeros_like(l_scr)
    acc_scr[...] = jnp.zeros_like(acc_scr)


def _mla_prompt_body(q_ref, kn_ref, kr_ref, v_ref, o_ref, m_scr, l_scr, acc_scr, *, tq, scale):
    qt = pl.program_id(2)
    qn = q_ref[:, 0:HEAD_DIM]
    qr = q_ref[:, HEAD_DIM:2 * HEAD_DIM]
    _softmax_init(m_scr, l_scr, acc_scr)

    def tile(kt, masked):
        start = pl.multiple_of(kt * tq, tq)
        kn = kn_ref[pl.ds(start, tq), :]
        kr = kr_ref[pl.ds(start, tq), :].astype(BF16)
        v = v_ref[pl.ds(start, tq), :]
        s = (_nt(qn, kn) + _nt(qr, kr)) * scale
        if masked:
            r = lax.broadcasted_iota(jnp.int32, s.shape, 0)
            c = lax.broadcasted_iota(jnp.int32, s.shape, 1)
            s = jnp.where(c <= r, s, NEG)
        alpha, (p,) = _softmax_step([s], m_scr, l_scr)
        acc_scr[...] = alpha * acc_scr[...] + _nn(p.astype(BF16), v)

    def body(kt, _):
        tile(kt, False)
        return 0

    lax.fori_loop(0, qt, body, 0)
    tile(qt, True)
    o_ref[...] = (acc_scr[...] / l_scr[...]).astype(o_ref.dtype)


def mla_prompt(q_cat, kv_up, down, *, B, T, n_heads, kr_block, tq=256):
    tq = _fit(T, tq)
    nqt = T // tq
    body = functools.partial(_mla_prompt_body, tq=tq, scale=(HEAD_DIM + HEAD_DIM // 2) ** -0.5)
    return pl.pallas_call(
        body,
        grid=(B, n_heads, nqt),
        in_specs=[pl.BlockSpec((tq, 2 * HEAD_DIM), lambda b, h, qt: (b * nqt + qt, h)),
                  pl.BlockSpec((T, HEAD_DIM), lambda b, h, qt: (b, h)),
                  pl.BlockSpec((T, LANES), lambda b, h, qt: (b, kr_block)),
                  pl.BlockSpec((T, HEAD_DIM), lambda b, h, qt: (b, n_heads + h))],
        out_specs=pl.BlockSpec((tq, HEAD_DIM), lambda b, h, qt: (b * nqt + qt, h)),
        out_shape=jax.ShapeDtypeStruct((B * T, n_heads * HEAD_DIM), BF16),
        scratch_shapes=[pltpu.VMEM((tq, 1), F32), pltpu.VMEM((tq, 1), F32), pltpu.VMEM((tq, HEAD_DIM), F32)],
        compiler_params=_params("parallel", "parallel", "arbitrary"),
        name="mla_prompt",
    )(q_cat, kv_up, down, kv_up)


def _mla_sample_body(pt_ref, ql_ref, qc_ref, new_ref, *rest, P, kv_lora, rope_dim, n_heads, scale):
    c_refs, r_refs = rest[:P], rest[P:2 * P]
    o_ref, m_scr, l_scr, acc_scr = rest[2 * P:2 * P + 4]
    c = pl.program_id(1)
    ql = ql_ref[...]
    qr = qc_ref[:, HEAD_DIM:HEAD_DIM + rope_dim]

    @pl.when(c == 0)
    def _():
        _softmax_init(m_scr, l_scr, acc_scr)
        ck = new_ref[:, 0:kv_lora].astype(BF16)
        kr = new_ref[:, kv_lora:kv_lora + rope_dim].astype(BF16)
        s = (_nt(ql, ck) + _nt(qr, kr)) * scale
        r = lax.broadcasted_iota(jnp.int32, s.shape, 0)
        col = lax.broadcasted_iota(jnp.int32, s.shape, 1)
        s = jnp.where(col <= _idiv(r, n_heads), s, NEG)
        alpha, (p,) = _softmax_step([s], m_scr, l_scr)
        acc_scr[...] = alpha * acc_scr[...] + _nn(p.astype(BF16), ck)

    cks = [c_refs[i][...].astype(BF16) for i in range(P)]
    s_list = [(_nt(ql, cks[i]) + _nn(qr, r_refs[i][...].astype(BF16))) * scale for i in range(P)]
    alpha, p_list = _softmax_step(s_list, m_scr, l_scr)
    acc = alpha * acc_scr[...]
    for i in range(P):
        acc = acc + _nn(p_list[i].astype(BF16), cks[i])
    acc_scr[...] = acc

    @pl.when(c == pl.num_programs(1) - 1)
    def _():
        o_ref[...] = (acc_scr[...] / l_scr[...]).astype(o_ref.dtype)


def mla_sample(q_lat, q_cat, new, cache_ckv, cache_kpe, layer, page_table, *, n_heads, P):
    B, rows, kv_lora = q_lat.shape
    rope_dim = cache_kpe.shape[2]
    n_pages = page_table.shape[1]
    P = _fit(n_pages, P)
    body = functools.partial(_mla_sample_body, P=P, kv_lora=kv_lora, rope_dim=rope_dim, n_heads=n_heads,
                             scale=(HEAD_DIM + rope_dim) ** -0.5)
    bspec = lambda r, w: pl.BlockSpec((None, r, w), lambda b, c, pt: (b, 0, 0))
    grid_spec = pltpu.PrefetchScalarGridSpec(
        num_scalar_prefetch=1,
        grid=(B, n_pages // P),
        in_specs=[bspec(rows, kv_lora), bspec(rows, 2 * HEAD_DIM), bspec(LANES, new.shape[-1])]
        + _page_specs(layer, n_pages, P, (LANES, kv_lora), False)
        + _page_specs(layer, n_pages, P, (rope_dim, LANES), False),
        out_specs=bspec(rows, kv_lora),
        scratch_shapes=[pltpu.VMEM((rows, 1), F32), pltpu.VMEM((rows, 1), F32), pltpu.VMEM((rows, kv_lora), F32)],
    )
    return pl.pallas_call(
        body,
        grid_spec=grid_spec,
        out_shape=jax.ShapeDtypeStruct((B, rows, kv_lora), BF16),
        compiler_params=_params("parallel", "arbitrary"),
        name="mla_sample",
    )(page_table, q_lat, q_cat, new, *([cache_ckv] * P), *([cache_kpe] * P))


def _top_blocks(gate, n_valid, n_sel):
    lane = lax.broadcasted_iota(jnp.int32, gate.shape, 1)
    g = jnp.where(lane < n_valid, gate, NEG)
    lane = lane.astype(F32)
    sel = jnp.zeros(gate.shape, F32)
    for r in range(n_sel):
        m = jnp.max(g, axis=1, keepdims=True)
        idx = jnp.min(jnp.where(g == m, lane, float(LANES)), axis=1, keepdims=True)
        pick = lane == idx
        sel = jnp.where(jnp.logical_and(pick, n_valid > r), 1.0, sel)
        g = jnp.where(pick, -jnp.inf, g)
    return sel


def _sel_column(sel, n):
    lane = lax.broadcasted_iota(jnp.int32, sel.shape, 1)
    return jnp.sum(jnp.where(lane == n, sel, 0.0), axis=1, keepdims=True) > 0.0


def _moba_prompt_body(q_ref, k_ref, v_ref, o_ref, km_scr, sel_scr, m_scr, l_scr, acc_scr, *, tq, n_group, n_blocks,
                      scale):
    qt = pl.program_id(2)

    @pl.when(qt == 0)
    def _():
        km_scr[...] = jnp.zeros_like(km_scr)
        for n in range(n_blocks):
            blk = k_ref[n * MOBA_BLOCK:(n + 1) * MOBA_BLOCK, :]
            km_scr[n:n + 1, :] = jnp.sum(blk, axis=0, keepdims=True) * (1.0 / MOBA_BLOCK)

    q32 = jnp.concatenate([q_ref[:, g * HEAD_DIM:(g + 1) * HEAD_DIM] for g in range(n_group)], axis=0)
    qs = q32.astype(BF16)
    rows = n_group * tq
    own = _idiv(qt * tq, MOBA_BLOCK)
    sel_scr[...] = _top_blocks(_nt3(q32, km_scr[...]), own, MOBA_TOPK)
    _softmax_init(m_scr, l_scr, acc_scr)

    def tile(n, valid):
        start = pl.multiple_of(n * MOBA_BLOCK, MOBA_BLOCK)
        k = k_ref[pl.ds(start, MOBA_BLOCK), :].astype(BF16)
        v = v_ref[pl.ds(start, MOBA_BLOCK), :].astype(BF16)
        s = jnp.where(valid, _nt(qs, k) * scale, NEG)
        alpha, (p,) = _softmax_step([s], m_scr, l_scr)
        acc_scr[...] = alpha * acc_scr[...] + _nn(p.astype(BF16), v)

    def body(n, _):
        tile(n, _sel_column(sel_scr[...], n))
        return 0

    lax.fori_loop(0, own, body, 0)
    r = lax.broadcasted_iota(jnp.int32, (rows, MOBA_BLOCK), 0)
    c = lax.broadcasted_iota(jnp.int32, (rows, MOBA_BLOCK), 1)
    tile(own, own * MOBA_BLOCK + c <= qt * tq + _imod(r, tq))
    out = acc_scr[...] / l_scr[...]
    for g in range(n_group):
        o_ref[:, g * HEAD_DIM:(g + 1) * HEAD_DIM] = out[g * tq:(g + 1) * tq, :].astype(o_ref.dtype)


def moba_prompt(qkv, *, B, T, n_heads, n_kv, tq=128):
    n_group = n_heads // n_kv
    assert MOBA_BLOCK % tq == 0 and T % MOBA_BLOCK == 0 and T // MOBA_BLOCK <= LANES
    nqt = T // tq
    gw = n_group * HEAD_DIM
    rows = n_group * tq
    body = functools.partial(_moba_prompt_body, tq=tq, n_group=n_group, n_blocks=T // MOBA_BLOCK,
                             scale=HEAD_DIM ** -0.5)
    return pl.pallas_call(
        body,
        grid=(B, n_kv, nqt),
        in_specs=[pl.BlockSpec((tq, gw), lambda b, kh, qt: (b * nqt + qt, kh)),
                  pl.BlockSpec((T, HEAD_DIM), lambda b, kh, qt: (b, n_heads + kh)),
                  pl.BlockSpec((T, HEAD_DIM), lambda b, kh, qt: (b, n_heads + n_kv + kh))],
        out_specs=pl.BlockSpec((tq, gw), lambda b, kh, qt: (b * nqt + qt, kh)),
        out_shape=jax.ShapeDtypeStruct((B * T, n_heads * HEAD_DIM), BF16),
        scratch_shapes=[pltpu.VMEM((LANES, HEAD_DIM), F32), pltpu.VMEM((rows, LANES), F32),
                        pltpu.VMEM((rows, 1), F32), pltpu.VMEM((rows, 1), F32), pltpu.VMEM((rows, HEAD_DIM), F32)],
        compiler_params=_params("arbitrary", "arbitrary", "arbitrary"),
        name="moba_prompt",
    )(qkv, qkv, qkv)


def _moba_select_body(pt_ref, q_ref, *rest, P, n_kv, n_past_blocks):
    k_refs = rest[:P]
    sel_ref, km_scr = rest[P], rest[P + 1]
    c = pl.program_id(1)
    pages_per_block = MOBA_BLOCK // LANES

    @pl.when(c == 0)
    def _():
        km_scr[...] = jnp.zeros_like(km_scr)

    row = lax.broadcasted_iota(jnp.int32, km_scr.shape, 0)
    km = km_scr[...]
    for i in range(P // pages_per_block):
        means = []
        for h in range(n_kv):
            tot = None
            for j in range(pages_per_block):
                page = k_refs[pages_per_block * i + j][pl.ds(h, LANES, stride=n_kv), :]
                tot = page if tot is None else tot + page
            means.append(jnp.sum(tot, axis=0, keepdims=True) * (1.0 / MOBA_BLOCK))
        mean = jnp.concatenate(means, axis=1)
        km = jnp.where(row == c * (P // pages_per_block) + i, mean, km)
    km_scr[...] = km

    @pl.when(c == pl.num_programs(1) - 1)
    def _():
        q = q_ref[...]
        rk = q.shape[0] // n_kv
        gate = jnp.concatenate([_nt3(q[h * rk:(h + 1) * rk], km[:, h * HEAD_DIM:(h + 1) * HEAD_DIM])
                                for h in range(n_kv)], axis=0)
        sel_ref[...] = _top_blocks(gate, n_past_blocks, MOBA_TOPK)


def moba_select(q, cache_k, layer, page_table, *, n_kv, P):
    B, rows, _ = q.shape
    n_pages = page_table.shape[1]
    P = _fit(n_pages, P)
    kvw = n_kv * HEAD_DIM
    n_past_blocks = n_pages * LANES // MOBA_BLOCK
    assert n_past_blocks <= LANES and P % (MOBA_BLOCK // LANES) == 0
    body = functools.partial(_moba_select_body, P=P, n_kv=n_kv, n_past_blocks=n_past_blocks)
    grid_spec = pltpu.PrefetchScalarGridSpec(
        num_scalar_prefetch=1,
        grid=(B, n_pages // P),
        in_specs=[pl.BlockSpec((None, rows, HEAD_DIM), lambda b, c, pt: (b, 0, 0))]
        + _page_specs(layer, n_pages, P, cache_k.shape[2:], False),
        out_specs=pl.BlockSpec((None, rows, LANES), lambda b, c, pt: (b, 0, 0)),
        scratch_shapes=[pltpu.VMEM((LANES, kvw), F32)],
    )
    return pl.pallas_call(
        body,
        grid_spec=grid_spec,
        out_shape=jax.ShapeDtypeStruct((B, rows, LANES), F32),
        compiler_params=_params("parallel", "arbitrary"),
        name="moba_select",
    )(page_table, q, *([cache_k] * P))


def _moba_sample_body(pt_ref, q_ref, sel_ref, kn_ref, vn_ref, *rest, P, n_kv, t_new, scale):
    k_refs, v_refs = rest[:P], rest[P:2 * P]
    o_ref, m_scr, l_scr, acc_scr = rest[2 * P:2 * P + 4]
    c = pl.program_id(1)
    q = q_ref[...].astype(BF16)
    rk = q.shape[0] // n_kv
    pages_per_block = MOBA_BLOCK // LANES

    def scores(ks):
        return jnp.concatenate([_nt(q[h * rk:(h + 1) * rk], ks[h]) for h in range(n_kv)], axis=0) * scale

    def pv(acc, p, vs):
        pb = p.astype(BF16)
        return acc + jnp.concatenate([_nn(pb[h * rk:(h + 1) * rk], vs[h]) for h in range(n_kv)], axis=0)

    @pl.when(c == 0)
    def _():
        _softmax_init(m_scr, l_scr, acc_scr)
        s = scores(_lane_heads(kn_ref[...], n_kv))
        r = lax.broadcasted_iota(jnp.int32, s.shape, 0)
        col = lax.broadcasted_iota(jnp.int32, s.shape, 1)
        s = jnp.where(col <= _imod(r, t_new), s, NEG)
        alpha, (p,) = _softmax_step([s], m_scr, l_scr)
        acc_scr[...] = pv(alpha * acc_scr[...], p, _lane_heads(vn_ref[...], n_kv))

    sel = sel_ref[...]
    s_list = []
    for i in range(P):
        valid = _sel_column(sel, _idiv(c * P + i, pages_per_block))
        s_list.append(jnp.where(valid, scores(_page_heads(k_refs[i], n_kv)), NEG))
    alpha, p_list = _softmax_step(s_list, m_scr, l_scr)
    acc = alpha * acc_scr[...]
    for i in range(P):
        acc = pv(acc, p_list[i], _page_heads(v_refs[i], n_kv))
    acc_scr[...] = acc

    @pl.when(c == pl.num_programs(1) - 1)
    def _():
        o_ref[...] = (acc_scr[...] / l_scr[...]).astype(o_ref.dtype)


def moba_sample(q, sel, k_new, v_new, cache_k, cache_v, layer, page_table, *, n_kv, t_new, P):
    B, rows, _ = q.shape
    n_pages = page_table.shape[1]
    P = _fit(n_pages, P)
    kvw = n_kv * HEAD_DIM
    body = functools.partial(_moba_sample_body, P=P, n_kv=n_kv, t_new=t_new, scale=HEAD_DIM ** -0.5)
    bspec = lambda r, w: pl.BlockSpec((None, r, w), lambda b, c, pt: (b, 0, 0))
    grid_spec = pltpu.PrefetchScalarGridSpec(
        num_scalar_prefetch=1,
        grid=(B, n_pages // P),
        in_specs=[bspec(rows, HEAD_DIM), bspec(rows, LANES), bspec(LANES, kvw), bspec(LANES, kvw)]
        + _page_specs(layer, n_pages, P, cache_k.shape[2:], False)
        + _page_specs(layer, n_pages, P, cache_v.shape[2:], False),
        out_specs=bspec(rows, HEAD_DIM),
        scratch_shapes=[pltpu.VMEM((rows, 1), F32), pltpu.VMEM((rows, 1), F32), pltpu.VMEM((rows, HEAD_DIM), F32)],
    )
    return pl.pallas_call(
        body,
        grid_spec=grid_spec,
        out_shape=jax.ShapeDtypeStruct((B, rows, HEAD_DIM), BF16),
        compiler_params=_params("parallel", "arbitrary"),
        name="moba_sample",
    )(page_table, q, sel, k_new, v_new, *([cache_k] * P), *([cache_v] * P))


def _heads_first(x, B, t_new, n_heads, width):
    return x.reshape(t_new, B, n_heads, width).transpose(1, 2, 0, 3).reshape(B, n_heads * t_new, width)


def _heads_last(o, B, t_new, n_heads, width):
    return o.reshape(B, n_heads, t_new, width).transpose(2, 0, 1, 3).reshape(t_new * B, n_heads * width)


def _batch_major(x, B, t_new):
    return x.reshape(t_new, B, x.shape[-1]).transpose(1, 0, 2)


def _time_major(x, B, t_new):
    k = x.shape[1] // t_new
    return x.reshape(B, t_new, k * x.shape[-1]).transpose(1, 0, 2).reshape(t_new * B, k * x.shape[-1])


def _pad_new(x_bt):
    return jnp.pad(x_bt, ((0, 0), (0, LANES - x_bt.shape[1]), (0, 0)))


TM_PROMPT = 1024
TM_FFN = 512
TN = 512
PAGES_PER_STEP = 8


def kernel(x_prompt, x_sample, cache_sb_k, cache_sb_v, cache_mla_ckv, cache_mla_kpe, cache_moba_k, cache_moba_v, page_table, c_prompt, c_sample, mod_w, mod_b, norm_mix, norm_ffn, final_norm, sb_wq, sb_wk, sb_wv, sb_wo, mla_wdq, mla_q_norm, mla_wuq, mla_wdkv, mla_kv_norm, mla_wuk, mla_wuv, mla_wo, moba_wq, moba_wk, moba_wv, moba_wo, ffn_wg, ffn_wu, ffn_wd, moe_wr, moe_br, moe_wg, moe_wu, moe_wd):
    Bp, T, D = x_prompt.shape
    Bs, Ts, _ = x_sample.shape
    depth = mod_w.shape[0]
    n_pages, page = page_table.shape[1], cache_sb_k.shape[2]
    past_len = n_pages * page
    assert page == LANES and Ts == SUBLANES and past_len % MOBA_BLOCK == 0
    n_heads = sb_wq.shape[2] // HEAD_DIM
    sb_kv = sb_wk.shape[2] // HEAD_DIM
    moba_kv = moba_wk.shape[2] // HEAD_DIM
    q_lora, kv_lora = mla_wdq.shape[2], mla_kv_norm.shape[1]
    rope_dim = mla_wdkv.shape[2] - kv_lora
    nope = mla_wuk.shape[3]
    assert nope == HEAD_DIM and mla_wuv.shape[3] == HEAD_DIM and rope_dim == HEAD_DIM // 2
    pos_p = jnp.arange(T, dtype=jnp.int32)
    pos_s = past_len + jnp.arange(Ts, dtype=jnp.int32)
    streams = ((Bp, T, Bp, T, pos_p, TM_PROMPT), (Ts, Bs, Bs, Ts, jnp.repeat(pos_s, Bs), Ts * Bs))

    n_c = Bs + Bp
    c_all = jnp.concatenate([c_sample, c_prompt, jnp.zeros((-n_c % SUBLANES, D), F32)], axis=0)
    mod_all = modulation_all(c_all, mod_w, mod_b).reshape(depth, c_all.shape[0], 6, D)

    flat_pages = lambda c: c.reshape(c.shape[0], c.shape[1], c.shape[2] * c.shape[3], c.shape[4])
    sb_cache_k, sb_cache_v = flat_pages(cache_sb_k), flat_pages(cache_sb_v)
    mb_cache_k, mb_cache_v = flat_pages(cache_moba_k), flat_pages(cache_moba_v)
    cache_kpe_t = cache_mla_kpe.transpose(0, 1, 3, 2)

    xs = [x_prompt, x_sample.transpose(1, 0, 2)]
    outs = {k: ([], []) for k in ("sb_k", "sb_v", "mla_c", "mla_r", "mb_k", "mb_v")}
    for i in range(depth):
        kind, j, f = i % N_MIXERS, i // N_MIXERS, i // 2
        mods = (mod_all[i, Bs:Bs + Bp].transpose(1, 0, 2).reshape(6, Bp, 1, D),
                mod_all[i, :Bs].transpose(1, 0, 2).reshape(6, 1, Bs, D))
        for s, (G, R, B, Tn, pos, tm) in enumerate(streams):
            x3, mod = xs[s], mods[s]
            is_sample = s == 1
            to_bt = (lambda y: _batch_major(y, B, Tn)) if is_sample else (lambda y: y.reshape(B, Tn, y.shape[-1]))
            if kind == 0:
                w = jnp.concatenate([sb_wq[j], sb_wk[j], sb_wv[j]], axis=1)
                qkv = ada_linear(x3, mod, 0, 1, norm_mix[i], w, tm=tm, tn=TN, name="sb_qkv")
                qw, kw = n_heads * HEAD_DIM, sb_kv * HEAD_DIM
                k_new, v_new = to_bt(qkv[:, qw:qw + kw]), to_bt(qkv[:, qw + kw:])
                outs["sb_k"][s].append(k_new.reshape(B, Tn, sb_kv, HEAD_DIM))
                outs["sb_v"][s].append(v_new.reshape(B, Tn, sb_kv, HEAD_DIM))
                if is_sample:
                    q = _heads_first(qkv[:, :qw], B, Tn, n_heads, HEAD_DIM)
                    o = sb_sample(q, _pad_new(k_new), _pad_new(v_new), sb_cache_k, sb_cache_v, j,
                                  page_table, n_kv=sb_kv, t_new=Tn, P=PAGES_PER_STEP)
                    o = _heads_last(o, B, Tn, n_heads, HEAD_DIM)
                else:
                    o = sb_prompt(qkv, B=B, T=Tn, n_heads=n_heads, n_kv=sb_kv)
                wo = sb_wo[j]
            elif kind == 1:
                half = rope_dim // 2
                c64, s1, s2 = _rope_tables(pos, rope_dim)
                w = jnp.concatenate([mla_wdq[j], mla_wdkv[j], jnp.zeros((D, LANES - rope_dim), F32)], axis=1)
                tmd = min(tm, TM_FFN)
                tabs, tab_specs = _row_tables((c64, s1, s2), G, R, tmd)
                down = ada_linear(
                    x3, mod, 0, 1, norm_mix[i], w, tm=tmd, tn=w.shape[1],
                    epilogue=functools.partial(_epi_mla_down, q_lora=q_lora, kv_lora=kv_lora, half=half),
                    extra=(mla_q_norm[j].reshape(1, q_lora), mla_kv_norm[j].reshape(1, kv_lora)) + tabs,
                    extra_specs=[_const_spec(q_lora), _const_spec(kv_lora)] + tab_specs,
                    name="mla_down")
                ckv = to_bt(down[:, q_lora:q_lora + kv_lora])
                kpe = to_bt(down[:, q_lora + kv_lora:q_lora + kv_lora + rope_dim])
                outs["mla_c"][s].append(ckv)
                outs["mla_r"][s].append(kpe)
                wuq = mla_wuq[j].reshape(q_lora, n_heads, nope + rope_dim)
                wuq = jnp.pad(wuq, ((0, 0), (0, 0), (0, 2 * HEAD_DIM - nope - rope_dim))).reshape(q_lora, -1)
                gb, rb, _ = _row_tiling(G, R, tm)
                tabs, tab_specs = _row_tables((c64, s1, s2), G, R, tm)
                q_cat = linear(down, wuq, x_cols=(q_lora, 0), tm=gb * rb, tn=TN, out_dtype=BF16,
                               epilogue=functools.partial(_epi_q_up, half=half), extra=tabs, extra_specs=tab_specs,
                               name="mla_q_up")
                wuk2 = mla_wuk[j].reshape(kv_lora, n_heads * nope)
                wuv2 = mla_wuv[j].reshape(kv_lora, n_heads * HEAD_DIM)
                if is_sample:
                    q_lat = head_linear(q_cat, wuk2, n_heads=n_heads, x_width=HEAD_DIM, x_stride=2, x_off=0,
                                        w_width=HEAD_DIM, transpose_w=True, out_width=kv_lora, tm=tm, name="mla_q_lat")
                    per_batch = lambda y, wd: _batch_major(y, B, Tn).reshape(B, Tn * n_heads, wd)
                    new = _pad_new(_batch_major(down[:, q_lora:], B, Tn))
                    o_lat = mla_sample(per_batch(q_lat, kv_lora), per_batch(q_cat, 2 * HEAD_DIM), new,
                                       cache_mla_ckv, cache_kpe_t, j, page_table, n_heads=n_heads,
                                       P=PAGES_PER_STEP)
                    o = head_linear(_time_major(o_lat, B, Tn), wuv2, n_heads=n_heads, x_width=kv_lora,
                                    x_stride=1, x_off=0, w_width=HEAD_DIM, transpose_w=False, out_width=HEAD_DIM,
                                    tm=tm, name="mla_o_up")
                else:
                    assert q_lora % kv_lora == 0
                    kv_up = linear(down, jnp.concatenate([wuk2, wuv2], axis=1), x_cols=(kv_lora, q_lora // kv_lora),
                                   tm=tm, tn=TN, out_dtype=BF16, name="mla_kv_up")
                    o = mla_prompt(q_cat, kv_up, down, B=B, T=Tn, n_heads=n_heads,
                                   kr_block=(q_lora + kv_lora) // LANES)
                wo = mla_wo[j]
            else:
                c128, s1, s2 = _rope_tables(pos, HEAD_DIM)
                tabs, tab_specs = _row_tables((c128, s1 + s2), G, R, tm)
                w = jnp.concatenate([moba_wq[j], moba_wk[j], moba_wv[j]], axis=1)
                qw, kw = n_heads * HEAD_DIM, moba_kv * HEAD_DIM
                tn = 2 * LANES
                qkv = ada_linear(x3, mod, 0, 1, norm_mix[i], w, tm=tm, tn=tn,
                                 epilogue=functools.partial(_epi_rope128, n_rope_tiles=(qw + kw) // tn),
                                 extra=tabs, extra_specs=tab_specs, name="moba_qkv")
                k_new, v_new = to_bt(qkv[:, qw:qw + kw]), to_bt(qkv[:, qw + kw:])
                outs["mb_k"][s].append(k_new.reshape(B, Tn, moba_kv, HEAD_DIM))
                outs["mb_v"][s].append(v_new.reshape(B, Tn, moba_kv, HEAD_DIM))
                if is_sample:
                    q = _heads_first(qkv[:, :qw], B, Tn, n_heads, HEAD_DIM)
                    sel = moba_select(q, mb_cache_k, j, page_table, n_kv=moba_kv, P=PAGES_PER_STEP)
                    o = moba_sample(q, sel, _pad_new(k_new), _pad_new(v_new), mb_cache_k, mb_cache_v, j,
                                    page_table, n_kv=moba_kv, t_new=Tn, P=PAGES_PER_STEP)
                    o = _heads_last(o, B, Tn, n_heads, HEAD_DIM)
                else:
                    o = moba_prompt(qkv, B=B, T=Tn, n_heads=n_heads, n_kv=moba_kv)
                wo = moba_wo[j]
            x3 = linear_res(o, wo, x3, mod, 2, tm=tm, tn=TN, name="mixer_out")
            tmf = min(tm, TM_FFN)
            if i % 2 == 0:
                x3 = ffn_dense(x3, mod, norm_ffn[i], ffn_wg[f], ffn_wu[f], ffn_wd[f], tm=tmf, tf=256)
            else:
                x3 = moe_dense(x3, mod, norm_ffn[i], moe_wr[f], moe_br[f], moe_wg[f], moe_wu[f], moe_wd[f],
                               tm=tmf, tf=256)
            xs[s] = x3
    y_prompt = rms_final(xs[0], final_norm, tm=TM_FFN)
    y_sample = rms_final(xs[1], final_norm, tm=TM_FFN).transpose(1, 0, 2)
    st = lambda key, s: jnp.stack(outs[key][s])
    return (y_prompt, y_sample,
            st("sb_k", 0), st("sb_v", 0), st("sb_k", 1), st("sb_v", 1),
            st("mla_c", 0), st("mla_r", 0), st("mla_c", 1), st("mla_r", 1),
            st("mb_k", 0), st("mb_v", 0), st("mb_k", 1), st("mb_v", 1))
```

```python
import functools

import jax
import jax.numpy as jnp
from jax import lax
from jax.experimental import pallas as pl
from jax.experimental.pallas import tpu as pltpu

F32 = jnp.float32
BF16 = jnp.bfloat16

LANES = 128
SUBLANES = 8
HEAD_DIM = 128
EPS = 1e-6
NEG = -1e30
ROPE_THETA = 10000.0
MOBA_BLOCK = 256
MOBA_TOPK = 3
N_MIXERS = 3
SB_DEAD = -104.0
VMEM_LIMIT = 56 << 20


def _params(*sem):
    return pltpu.CompilerParams(dimension_semantics=sem, vmem_limit_bytes=VMEM_LIMIT)


def _nt(a, b):
    return lax.dot_general(a, b, (((1,), (1,)), ((), ())), preferred_element_type=F32)


def _nn(a, b):
    return jnp.dot(a, b, preferred_element_type=F32)


def _split(x):
    hi = x.astype(BF16)
    lo = (x - hi.astype(F32)).astype(BF16)
    return hi, lo


def _nt3(a, b):
    ah, al = _split(a)
    bh, bl = _split(b)
    return _nt(ah, bh) + _nt(ah, bl) + _nt(al, bh)


def _softplus(z):
    return jnp.maximum(z, 0.0) + jnp.log(1.0 + jnp.exp(-jnp.abs(z)))


def _silu(g):
    return g * (1.0 / (1.0 + jnp.exp(-g)))


def _rms(x, g):
    return x * lax.rsqrt(jnp.mean(x * x, axis=-1, keepdims=True) + EPS) * g


def _row_tiling(G, R, tm):
    if R >= tm:
        assert R % tm == 0
        return 1, tm, R // tm
    assert R % SUBLANES == 0
    gb = max(min(tm // R, G), 1)
    while G % gb:
        gb -= 1
    return gb, R, 1


def _log2(n):
    assert n > 0 and n & (n - 1) == 0, n
    return n.bit_length() - 1


def _imod(x, n):
    _log2(n)
    return x & (n - 1)


def _idiv(x, n):
    return x >> _log2(n)


def _mod_spec(mod, j, gb, rb, nrb, width, col=None):
    _, gm, rm, _ = mod.shape
    assert gm == 1 or rm == 1
    block = (None, gb if gm > 1 else 1, rb if rm > 1 else 1, width)

    def imap(*ids):
        i = ids[0]
        return (j, i // nrb if gm > 1 else 0, i % nrb if rm > 1 else 0, 0 if col is None else ids[col])

    return pl.BlockSpec(block, imap)


def _fit(n, want):
    t = min(n, want)
    while n % t:
        t -= LANES if t > LANES else 1
    return t


def _rope_tables(pos, dim):
    half = dim // 2
    inv_freq = ROPE_THETA ** (-jnp.arange(half, dtype=F32) / half)
    ang = pos.astype(F32)[:, None] * inv_freq[None, :]
    cos, sin = jnp.cos(ang), jnp.sin(ang)
    z = jnp.zeros((pos.shape[0], LANES - dim), F32)
    zh = jnp.zeros_like(cos)
    c = jnp.concatenate([cos, cos, z], axis=1)
    s1 = jnp.concatenate([-sin, zh, z], axis=1)
    s2 = jnp.concatenate([zh, sin, z], axis=1)
    return c, s1, s2


def _rope128(a, c, s):
    return a * c + pltpu.roll(a, HEAD_DIM // 2, axis=1) * s


def _rope64(a, c, s1, s2, half):
    return a * c + pltpu.roll(a, LANES - half, axis=1) * s1 + pltpu.roll(a, half, axis=1) * s2


def _mod_body(c_ref, w_ref, b_ref, o_ref):
    a = _silu(c_ref[...]).astype(BF16)
    o_ref[...] = _nn(a, w_ref[...].astype(BF16)) + b_ref[...]


def modulation_all(c_all, mod_w, mod_b):
    L, D, N = mod_w.shape
    Bp = c_all.shape[0]
    tn = _fit(N, 1024)
    return pl.pallas_call(
        _mod_body,
        grid=(L, N // tn),
        in_specs=[pl.BlockSpec((Bp, D), lambda l, n: (0, 0)),
                  pl.BlockSpec((None, D, tn), lambda l, n: (l, 0, n)),
                  pl.BlockSpec((None, 1, tn), lambda l, n: (l, 0, n))],
        out_specs=pl.BlockSpec((None, Bp, tn), lambda l, n: (l, 0, n)),
        out_shape=jax.ShapeDtypeStruct((L, Bp, N), F32),
        compiler_params=_params("parallel", "parallel"),
        name="modulation",
    )(c_all, mod_w, mod_b.reshape(L, 1, N))


def _ada_h(x_ref, sh_ref, sc_ref, g_ref):
    x = x_ref[...]
    h = _rms(x, g_ref[...]) * (1.0 + sc_ref[...]) + sh_ref[...]
    return h.reshape(h.shape[0] * h.shape[1], h.shape[2])


def _ada_linear_body(x_ref, sh_ref, sc_ref, g_ref, w_ref, *rest, epilogue, n_extra):
    extra, o_ref, h_scr = rest[:n_extra], rest[n_extra], rest[n_extra + 1]
    n = pl.program_id(1)

    @pl.when(n == 0)
    def _():
        h_scr[...] = _ada_h(x_ref, sh_ref, sc_ref, g_ref).astype(BF16)

    acc = _nn(h_scr[...], w_ref[...].astype(BF16))
    epilogue(acc, n, extra, o_ref)


def _epi_plain(acc, n, extra, o_ref):
    o_ref[...] = acc.astype(o_ref.dtype)


def _epi_rope128(acc, n, extra, o_ref, *, n_rope_tiles):
    c_ref, s_ref = extra

    @pl.when(n < n_rope_tiles)
    def _():
        c, s = c_ref[...], s_ref[...]
        for j in range(acc.shape[1] // LANES):
            sl = slice(j * LANES, (j + 1) * LANES)
            o_ref[:, sl] = _rope128(acc[:, sl], c, s).astype(o_ref.dtype)

    @pl.when(n >= n_rope_tiles)
    def _():
        o_ref[...] = acc.astype(o_ref.dtype)


def _epi_mla_down(acc, n, extra, o_ref, *, q_lora, kv_lora, half):
    qn_ref, kn_ref, c_ref, s1_ref, s2_ref = extra
    o_ref[:, 0:q_lora] = _rms(acc[:, 0:q_lora], qn_ref[...])
    o_ref[:, q_lora:q_lora + kv_lora] = _rms(acc[:, q_lora:q_lora + kv_lora], kn_ref[...])
    r0 = q_lora + kv_lora
    o_ref[:, r0:r0 + LANES] = _rope64(acc[:, r0:r0 + LANES], c_ref[...], s1_ref[...], s2_ref[...], half)


def ada_linear(x3, mod, j_shift, j_scale, gain, w, *, tm, tn, epilogue=_epi_plain, extra=(), extra_specs=(),
               out_dtype=F32, name):
    G, R, K = x3.shape
    N = w.shape[1]
    gb, rb, nrb = _row_tiling(G, R, tm)
    tm = gb * rb
    tn = _fit(N, tn)
    body = functools.partial(_ada_linear_body, epilogue=epilogue, n_extra=len(extra))
    mspec = lambda j: _mod_spec(mod, j, gb, rb, nrb, K)
    return pl.pallas_call(
        body,
        grid=(G * R // tm, N // tn),
        in_specs=[pl.BlockSpec((gb, rb, K), lambda i, n: (i // nrb, i % nrb, 0)),
                  mspec(j_shift), mspec(j_scale),
                  pl.BlockSpec((1, K), lambda i, n: (0, 0)),
                  pl.BlockSpec((K, tn), lambda i, n: (0, n))] + list(extra_specs),
        out_specs=pl.BlockSpec((tm, tn), lambda i, n: (i, n)),
        out_shape=jax.ShapeDtypeStruct((G * R, N), out_dtype),
        scratch_shapes=[pltpu.VMEM((tm, K), BF16)],
        compiler_params=_params("parallel", "arbitrary"),
        name=name,
    )(x3, mod, mod, gain.reshape(1, K), w, *extra)


def _row_tables(tabs, G, R, tm):
    gb, rb, _ = _row_tiling(G, R, tm)
    tm = gb * rb
    n = tabs[0].shape[0]
    if n < tm:
        assert tm % n == 0
        tabs = tuple(jnp.tile(t, (tm // n, 1)) for t in tabs)
        n = tm
    assert n % tm == 0
    return tabs, [pl.BlockSpec((tm, LANES), lambda i, c: (i % (n // tm), 0)) for _ in tabs]


def _const_spec(width):
    return pl.BlockSpec((1, width), lambda i, n: (0, 0))


def _linear_body(x_ref, w_ref, *rest, epilogue, n_extra, transpose_w):
    extra, o_ref = rest[:n_extra], rest[n_extra]
    x = x_ref[...].astype(BF16)
    w = w_ref[...].astype(BF16)
    acc = _nt(x, w) if transpose_w else _nn(x, w)
    epilogue(acc, pl.program_id(1), extra, o_ref)


def _epi_q_up(acc, n, extra, o_ref, *, half):
    c_ref, s1_ref, s2_ref = extra
    for j in range(acc.shape[1] // LANES):
        sl = slice(j * LANES, (j + 1) * LANES)
        a = acc[:, sl]
        if j % 2:
            a = _rope64(a, c_ref[...], s1_ref[...], s2_ref[...], half)
        o_ref[:, sl] = a.astype(o_ref.dtype)


def linear(x, w, *, x_cols, tm, tn, out_dtype, epilogue=_epi_plain, extra=(), extra_specs=(), name):
    M = x.shape[0]
    kw, kb = x_cols
    N = w.shape[1]
    tm, tn = _fit(M, tm), _fit(N, tn)
    body = functools.partial(_linear_body, epilogue=epilogue, n_extra=len(extra), transpose_w=False)
    return pl.pallas_call(
        body,
        grid=(M // tm, N // tn),
        in_specs=[pl.BlockSpec((tm, kw), lambda i, n: (i, kb)),
                  pl.BlockSpec((kw, tn), lambda i, n: (0, n))] + list(extra_specs),
        out_specs=pl.BlockSpec((tm, tn), lambda i, n: (i, n)),
        out_shape=jax.ShapeDtypeStruct((M, N), out_dtype),
        compiler_params=_params("parallel", "arbitrary"),
        name=name,
    )(x, w, *extra)


def head_linear(x, w, *, n_heads, x_width, x_stride, x_off, w_width, transpose_w, out_width, tm, name):
    M = x.shape[0]
    tm = _fit(M, tm)
    body = functools.partial(_linear_body, epilogue=_epi_plain, n_extra=0, transpose_w=transpose_w)
    return pl.pallas_call(
        body,
        grid=(M // tm, n_heads),
        in_specs=[pl.BlockSpec((tm, x_width), lambda i, h: (i, h * x_stride + x_off)),
                  pl.BlockSpec((w.shape[0], w_width), lambda i, h: (0, h))],
        out_specs=pl.BlockSpec((tm, out_width), lambda i, h: (i, h)),
        out_shape=jax.ShapeDtypeStruct((M, n_heads * out_width), BF16),
        compiler_params=_params("parallel", "arbitrary"),
        name=name,
    )(x, w)


def _linear_res_body(a_ref, w_ref, x_ref, gate_ref, o_ref):
    acc = _nn(a_ref[...].astype(BF16), w_ref[...].astype(BF16))
    o_ref[...] = x_ref[...] + gate_ref[...] * acc.reshape(x_ref.shape)


def linear_res(a, w, x3, mod, j_gate, *, tm, tn, name):
    G, R, N = x3.shape
    K = a.shape[1]
    gb, rb, nrb = _row_tiling(G, R, tm)
    tm = gb * rb
    tn = _fit(N, tn)
    return pl.pallas_call(
        _linear_res_body,
        grid=(G * R // tm, N // tn),
        in_specs=[pl.BlockSpec((tm, K), lambda i, n: (i, 0)),
                  pl.BlockSpec((K, tn), lambda i, n: (0, n)),
                  pl.BlockSpec((gb, rb, tn), lambda i, n: (i // nrb, i % nrb, n)),
                  _mod_spec(mod, j_gate, gb, rb, nrb, tn, col=1)],
        out_specs=pl.BlockSpec((gb, rb, tn), lambda i, n: (i // nrb, i % nrb, n)),
        out_shape=jax.ShapeDtypeStruct(x3.shape, F32),
        compiler_params=_params("parallel", "arbitrary"),
        name=name,
    )(a, w, x3, mod)


def _ffn_body(x_ref, sh_ref, sc_ref, gate_ref, g_ref, wg_ref, wu_ref, wd_ref, o_ref, h_scr, acc_scr):
    f = pl.program_id(1)

    @pl.when(f == 0)
    def _():
        h_scr[...] = _ada_h(x_ref, sh_ref, sc_ref, g_ref).astype(BF16)
        acc_scr[...] = jnp.zeros_like(acc_scr)

    h = h_scr[...]
    a = _silu(_nn(h, wg_ref[...].astype(BF16))) * _nn(h, wu_ref[...].astype(BF16))
    acc_scr[...] += _nn(a.astype(BF16), wd_ref[...].astype(BF16))

    @pl.when(f == pl.num_programs(1) - 1)
    def _():
        o_ref[...] = x_ref[...] + gate_ref[...] * acc_scr[...].reshape(x_ref.shape)


def ffn_dense(x3, mod, gain, wg, wu, wd, layer, *, tm, tf):
    G, R, D = x3.shape
    Fd = wg.shape[2]
    gb, rb, nrb = _row_tiling(G, R, tm)
    tm = gb * rb
    tf = _fit(Fd, tf)
    mspec = lambda j: _mod_spec(mod, j, gb, rb, nrb, D)
    xspec = pl.BlockSpec((gb, rb, D), lambda i, f: (i // nrb, i % nrb, 0))
    return pl.pallas_call(
        _ffn_body,
        grid=(G * R // tm, Fd // tf),
        in_specs=[xspec, mspec(3), mspec(4), mspec(5),
                  pl.BlockSpec((1, D), lambda i, f: (0, 0)),
                  pl.BlockSpec((None, D, tf), lambda i, f: (layer, 0, f)),
                  pl.BlockSpec((None, D, tf), lambda i, f: (layer, 0, f)),
                  pl.BlockSpec((None, tf, D), lambda i, f: (layer, f, 0))],
        out_specs=xspec,
        out_shape=jax.ShapeDtypeStruct(x3.shape, F32),
        scratch_shapes=[pltpu.VMEM((tm, D), BF16), pltpu.VMEM((tm, D), F32)],
        compiler_params=_params("parallel", "arbitrary"),
        name="ffn_dense",
    )(x3, mod, mod, mod, gain.reshape(1, D), wg, wu, wd)


def _top2_combine(logits, lane):
    m1 = jnp.max(logits, axis=1, keepdims=True)
    i1 = jnp.min(jnp.where(logits == m1, lane, float(LANES)), axis=1, keepdims=True)
    rest = jnp.where(lane == i1, -jnp.inf, logits)
    m2 = jnp.max(rest, axis=1, keepdims=True)
    i2 = jnp.min(jnp.where(rest == m2, lane, float(LANES)), axis=1, keepdims=True)
    e = jnp.exp(m2 - m1)
    den = 1.0 + e
    return jnp.where(lane == i1, 1.0 / den, 0.0) + jnp.where(lane == i2, e / den, 0.0)


def _moe_body(x_ref, sh_ref, sc_ref, gate_ref, g_ref, wr_ref, br_ref, wg_ref, wu_ref, wd_ref, o_ref,
              h_scr, acc_scr, comb_scr):
    e, f = pl.program_id(1), pl.program_id(2)
    first = jnp.logical_and(e == 0, f == 0)
    last = jnp.logical_and(e == pl.num_programs(1) - 1, f == pl.num_programs(2) - 1)

    @pl.when(first)
    def _():
        h = _ada_h(x_ref, sh_ref, sc_ref, g_ref)
        h_scr[...] = h.astype(BF16)
        acc_scr[...] = jnp.zeros_like(acc_scr)
        hh, hl = _split(h)
        wh, wl = _split(wr_ref[...])
        logits = _nn(hh, wh) + _nn(hh, wl) + _nn(hl, wh) + br_ref[...]
        lane = lax.broadcasted_iota(jnp.int32, logits.shape, 1).astype(F32)
        comb_scr[...] = _top2_combine(logits, lane)

    comb = comb_scr[...]
    lane = lax.broadcasted_iota(jnp.int32, comb.shape, 1)
    ce = jnp.sum(jnp.where(lane == e, comb, 0.0), axis=1, keepdims=True)
    h = h_scr[...]
    a = _silu(_nn(h, wg_ref[...].astype(BF16))) * _nn(h, wu_ref[...].astype(BF16))
    acc_scr[...] += _nn((a * ce).astype(BF16), wd_ref[...].astype(BF16))

    @pl.when(last)
    def _():
        o_ref[...] = x_ref[...] + gate_ref[...] * acc_scr[...].reshape(x_ref.shape)


def moe_dense(x3, mod, gain, wr, br, wg, wu, wd, layer, *, tm, tf):
    G, R, D = x3.shape
    _, E, _, Fe = wg.shape
    gb, rb, nrb = _row_tiling(G, R, tm)
    tm = gb * rb
    tf = _fit(Fe, tf)
    wr_pad = jnp.zeros((D, LANES), F32).at[:, :E].set(wr)
    br_pad = jnp.full((1, LANES), NEG, F32).at[0, :E].set(br)
    mspec = lambda j: _mod_spec(mod, j, gb, rb, nrb, D)
    xspec = pl.BlockSpec((gb, rb, D), lambda i, e, f: (i // nrb, i % nrb, 0))
    return pl.pallas_call(
        _moe_body,
        grid=(G * R // tm, E, Fe // tf),
        in_specs=[xspec, mspec(3), mspec(4), mspec(5),
                  pl.BlockSpec((1, D), lambda i, e, f: (0, 0)),
                  pl.BlockSpec((D, LANES), lambda i, e, f: (0, 0)),
                  pl.BlockSpec((1, LANES), lambda i, e, f: (0, 0)),
                  pl.BlockSpec((None, None, D, tf), lambda i, e, f: (layer, e, 0, f)),
                  pl.BlockSpec((None, None, D, tf), lambda i, e, f: (layer, e, 0, f)),
                  pl.BlockSpec((None, None, tf, D), lambda i, e, f: (layer, e, f, 0))],
        out_specs=xspec,
        out_shape=jax.ShapeDtypeStruct(x3.shape, F32),
        scratch_shapes=[pltpu.VMEM((tm, D), BF16), pltpu.VMEM((tm, D), F32), pltpu.VMEM((tm, LANES), F32)],
        compiler_params=_params("parallel", "arbitrary", "arbitrary"),
        name="moe_dense",
    )(x3, mod, mod, mod, gain.reshape(1, D), wr_pad, br_pad, wg, wu, wd)


def _final_norm_body(x_ref, g_ref, o_ref):
    o_ref[...] = _rms(x_ref[...], g_ref[...])


def rms_final(x3, gain, *, tm):
    G, R, D = x3.shape
    gb, rb, nrb = _row_tiling(G, R, tm)
    spec = pl.BlockSpec((gb, rb, D), lambda i: (i // nrb, i % nrb, 0))
    return pl.pallas_call(
        _final_norm_body,
        grid=(G * R // (gb * rb),),
        in_specs=[spec, pl.BlockSpec((1, D), lambda i: (0, 0))],
        out_specs=spec,
        out_shape=jax.ShapeDtypeStruct(x3.shape, F32),
        compiler_params=_params("parallel"),
        name="final_norm",
    )(x3, gain.reshape(1, D))


def _cumsum_matrix(tk):
    j = lax.broadcasted_iota(jnp.int32, (2 * tk, tk), 0)
    s = lax.broadcasted_iota(jnp.int32, (2 * tk, tk), 1)
    jj = jnp.where(j >= tk, j - tk, j)
    return jnp.where(jj > s, 1.0, 0.0).astype(BF16)


def _sb_tile(z, past, carry, u2):
    sp = _softplus(z)
    log_beta = z - sp
    log_stay = -sp
    if past is not None:
        log_stay = jnp.where(past, log_stay, 0.0)
    hi, lo = _split(log_stay)
    local = _nn(jnp.concatenate([hi, lo], axis=1), u2)
    w = jnp.exp(log_beta + local + carry)
    if past is not None:
        w = jnp.where(past, w, 0.0)
    return w, jnp.sum(log_stay, axis=1, keepdims=True)


def _sb_live(carry):
    return jnp.max(carry) > SB_DEAD


def _sb_prompt_body(q_ref, k_ref, v_ref, o_ref, qs_scr, acc_scr, carry_scr, *, tq, n_group, scale):
    qt = pl.program_id(2)
    for g in range(n_group):
        qs_scr[g * tq:(g + 1) * tq, :] = q_ref[:, g * HEAD_DIM:(g + 1) * HEAD_DIM].astype(BF16)
    u2 = _cumsum_matrix(tq)
    rows = n_group * tq

    def tile(kt, masked):
        start = pl.multiple_of(kt * tq, tq)
        k = k_ref[pl.ds(start, tq), :].astype(BF16)
        v = v_ref[pl.ds(start, tq), :].astype(BF16)
        z = _nt(qs_scr[...], k) * scale
        past = None
        if masked:
            r = lax.broadcasted_iota(jnp.int32, (rows, tq), 0)
            c = lax.broadcasted_iota(jnp.int32, (rows, tq), 1)
            past = c < _imod(r, tq)
        w, tot = _sb_tile(z, past, carry_scr[...], u2)
        acc_scr[...] += _nn(w.astype(BF16), v)
        carry_scr[...] += tot

    acc_scr[...] = jnp.zeros_like(acc_scr)
    carry_scr[...] = jnp.zeros_like(carry_scr)
    tile(qt, True)

    def cond(j):
        return jnp.logical_and(j < qt, _sb_live(carry_scr[...]))

    def body(j):
        tile(qt - 1 - j, False)
        return j + 1

    lax.while_loop(cond, body, 0)
    for g in range(n_group):
        o_ref[:, g * HEAD_DIM:(g + 1) * HEAD_DIM] = acc_scr[g * tq:(g + 1) * tq, :].astype(o_ref.dtype)


def sb_prompt(qkv, *, B, T, n_heads, n_kv, tq=128):
    n_group = n_heads // n_kv
    nqt = T // tq
    gw = n_group * HEAD_DIM
    body = functools.partial(_sb_prompt_body, tq=tq, n_group=n_group, scale=HEAD_DIM ** -0.5)
    return pl.pallas_call(
        body,
        grid=(B, n_kv, nqt),
        in_specs=[pl.BlockSpec((tq, gw), lambda b, kh, qt: (b * nqt + qt, kh)),
                  pl.BlockSpec((T, HEAD_DIM), lambda b, kh, qt: (b, n_heads + kh)),
                  pl.BlockSpec((T, HEAD_DIM), lambda b, kh, qt: (b, n_heads + n_kv + kh))],
        out_specs=pl.BlockSpec((tq, gw), lambda b, kh, qt: (b * nqt + qt, kh)),
        out_shape=jax.ShapeDtypeStruct((B * T, n_heads * HEAD_DIM), BF16),
        scratch_shapes=[pltpu.VMEM((n_group * tq, HEAD_DIM), BF16), pltpu.VMEM((n_group * tq, HEAD_DIM), F32),
                        pltpu.VMEM((n_group * tq, 1), F32)],
        compiler_params=_params("parallel", "parallel", "arbitrary"),
        name="sb_prompt",
    )(qkv, qkv, qkv)


def _page_specs(layer, n_pages, P, page_shape, reverse):
    def spec(i):
        def imap(b, c, pt):
            p = c * P + i
            if reverse:
                p = n_pages - 1 - p
            return (layer, pt[b, p]) + (0,) * len(page_shape)
        return pl.BlockSpec((None, None) + tuple(page_shape), imap)
    return [spec(i) for i in range(P)]


def _page_heads(ref, n_kv):
    page = ref.shape[0] // n_kv
    return [ref[pl.ds(h, page, stride=n_kv), :].astype(BF16) for h in range(n_kv)]


def _lane_heads(x, n_kv):
    return [x[:, h * HEAD_DIM:(h + 1) * HEAD_DIM].astype(BF16) for h in range(n_kv)]


def _sb_sample_body(pt_ref, q_ref, kn_ref, vn_ref, *rest, P, n_kv, t_new, scale):
    k_refs, v_refs = rest[:P], rest[P:2 * P]
    o_ref, acc_scr, carry_scr = rest[2 * P], rest[2 * P + 1], rest[2 * P + 2]
    c = pl.program_id(1)
    q = q_ref[...].astype(BF16)
    rows = q.shape[0]
    rk = rows // n_kv
    u2 = _cumsum_matrix(LANES)

    def tile(ks, vs, masked, carry, acc):
        z = jnp.concatenate([_nt(q[h * rk:(h + 1) * rk], ks[h]) for h in range(n_kv)], axis=0) * scale
        past = None
        if masked:
            r = lax.broadcasted_iota(jnp.int32, z.shape, 0)
            col = lax.broadcasted_iota(jnp.int32, z.shape, 1)
            past = col < _imod(r, t_new)
        w, tot = _sb_tile(z, past, carry, u2)
        wb = w.astype(BF16)
        pv = jnp.concatenate([_nn(wb[h * rk:(h + 1) * rk], vs[h]) for h in range(n_kv)], axis=0)
        return carry + tot, acc + pv

    @pl.when(c == 0)
    def _():
        carry, acc = tile(_lane_heads(kn_ref[...], n_kv), _lane_heads(vn_ref[...], n_kv), True,
                          jnp.zeros(carry_scr.shape, F32), jnp.zeros(acc_scr.shape, F32))
        carry_scr[...] = carry
        acc_scr[...] = acc

    @pl.when(_sb_live(carry_scr[...]))
    def _():
        carry, acc = carry_scr[...], acc_scr[...]
        for i in range(P):
            carry, acc = tile(_page_heads(k_refs[i], n_kv), _page_heads(v_refs[i], n_kv), False, carry, acc)
        carry_scr[...] = carry
        acc_scr[...] = acc

    @pl.when(c == pl.num_programs(1) - 1)
    def _():
        o_ref[...] = acc_scr[...].astype(o_ref.dtype)


def sb_sample(q, k_new, v_new, cache_k, cache_v, layer, page_table, *, n_kv, t_new, P):
    B, rows, _ = q.shape
    n_pages = page_table.shape[1]
    P = _fit(n_pages, P)
    kvw = n_kv * HEAD_DIM
    page_shape = cache_k.shape[2:]
    body = functools.partial(_sb_sample_body, P=P, n_kv=n_kv, t_new=t_new, scale=HEAD_DIM ** -0.5)
    bspec = lambda w: pl.BlockSpec((None, rows if w == HEAD_DIM else LANES, w), lambda b, c, pt: (b, 0, 0))
    grid_spec = pltpu.PrefetchScalarGridSpec(
        num_scalar_prefetch=1,
        grid=(B, n_pages // P),
        in_specs=[bspec(HEAD_DIM), bspec(kvw), bspec(kvw)]
        + _page_specs(layer, n_pages, P, page_shape, True) + _page_specs(layer, n_pages, P, page_shape, True),
        out_specs=pl.BlockSpec((None, rows, HEAD_DIM), lambda b, c, pt: (b, 0, 0)),
        scratch_shapes=[pltpu.VMEM((rows, HEAD_DIM), F32), pltpu.VMEM((rows, 1), F32)],
    )
    return pl.pallas_call(
        body,
        grid_spec=grid_spec,
        out_shape=jax.ShapeDtypeStruct((B, rows, HEAD_DIM), BF16),
        compiler_params=_params("parallel", "arbitrary"),
        name="sb_sample",
    )(page_table, q, k_new, v_new, *([cache_k] * P), *([cache_v] * P))


def _softmax_step(s_list, m_scr, l_scr):
    m_old = m_scr[...]
    m_new = m_old
    for s in s_list:
        m_new = jnp.maximum(m_new, jnp.max(s, axis=1, keepdims=True))
    alpha = jnp.exp(m_old - m_new)
    p_list = [jnp.exp(s - m_new) for s in s_list]
    l = alpha * l_scr[...]
    for p in p_list:
        l = l + jnp.sum(p, axis=1, keepdims=True)
    m_scr[...] = m_new
    l_scr[...] = l
    return alpha, p_list


def _softmax_init(m_scr, l_scr, acc_scr):
    m_scr[...] = jnp.full(m_scr.shape, NEG, F32)
    l_scr[...] = jnp.zeros_like(l_scr)
    acc_scr[...] = jnp.zeros_like(acc_scr)


def _mla_prompt_body(q_ref, kn_ref, kr_ref, v_ref, o_ref, m_scr, l_scr, acc_scr, *, tq, tk, scale):
    qt = pl.program_id(2)
    qn = q_ref[:, 0:HEAD_DIM]
    qr = q_ref[:, HEAD_DIM:2 * HEAD_DIM]
    _softmax_init(m_scr, l_scr, acc_scr)

    def chunk(ct, masked):
        start = pl.multiple_of(ct * tk, tk)
        kn = kn_ref[pl.ds(start, tk), :]
        kr = kr_ref[pl.ds(start, tk), :].astype(BF16)
        v = v_ref[pl.ds(start, tk), :]
        s = (_nt(qn, kn) + _nt(qr, kr)) * scale
        if masked:
            r = lax.broadcasted_iota(jnp.int32, s.shape, 0)
            c = lax.broadcasted_iota(jnp.int32, s.shape, 1)
            s = jnp.where(ct * tk + c <= qt * tq + r, s, NEG)
        alpha, (p,) = _softmax_step([s], m_scr, l_scr)
        acc_scr[...] = alpha * acc_scr[...] + _nn(p.astype(BF16), v)

    def body(ct, _):
        chunk(ct, False)
        return 0

    n_full = (qt * tq) // tk
    lax.fori_loop(0, n_full, body, 0)
    chunk(n_full, True)
    o_ref[...] = (acc_scr[...] / l_scr[...]).astype(o_ref.dtype)


def mla_prompt(q_cat, kv_up, down, *, B, T, n_heads, kr_block, tq=256, tk=1024):
    tq = _fit(T, tq)
    tk = _fit(T, tk)
    assert tk % tq == 0
    nqt = T // tq
    body = functools.partial(_mla_prompt_body, tq=tq, tk=tk, scale=(HEAD_DIM + HEAD_DIM // 2) ** -0.5)
    return pl.pallas_call(
        body,
        grid=(B, n_heads, nqt),
        in_specs=[pl.BlockSpec((tq, 2 * HEAD_DIM), lambda b, h, qt: (b * nqt + qt, h)),
                  pl.BlockSpec((T, HEAD_DIM), lambda b, h, qt: (b, h)),
                  pl.BlockSpec((T, LANES), lambda b, h, qt: (b, kr_block)),
                  pl.BlockSpec((T, HEAD_DIM), lambda b, h, qt: (b, n_heads + h))],
        out_specs=pl.BlockSpec((tq, HEAD_DIM), lambda b, h, qt: (b * nqt + qt, h)),
        out_shape=jax.ShapeDtypeStruct((B * T, n_heads * HEAD_DIM), BF16),
        scratch_shapes=[pltpu.VMEM((tq, 1), F32), pltpu.VMEM((tq, 1), F32), pltpu.VMEM((tq, HEAD_DIM), F32)],
        compiler_params=_params("parallel", "parallel", "arbitrary"),
        name="mla_prompt",
    )(q_cat, kv_up, down, kv_up)


def _mla_sample_body(pt_ref, ql_ref, qc_ref, new_ref, *rest, P, kv_lora, rope_dim, n_heads, scale):
    c_refs, r_refs = rest[:P], rest[P:2 * P]
    o_ref, m_scr, l_scr, acc_scr = rest[2 * P:2 * P + 4]
    c = pl.program_id(1)
    ql = ql_ref[...]
    qr = qc_ref[:, HEAD_DIM:HEAD_DIM + rope_dim]

    @pl.when(c == 0)
    def _():
        _softmax_init(m_scr, l_scr, acc_scr)
        ck = new_ref[:, 0:kv_lora].astype(BF16)
        kr = new_ref[:, kv_lora:kv_lora + rope_dim].astype(BF16)
        s = (_nt(ql, ck) + _nt(qr, kr)) * scale
        r = lax.broadcasted_iota(jnp.int32, s.shape, 0)
        col = lax.broadcasted_iota(jnp.int32, s.shape, 1)
        s = jnp.where(col <= _idiv(r, n_heads), s, NEG)
        alpha, (p,) = _softmax_step([s], m_scr, l_scr)
        acc_scr[...] = alpha * acc_scr[...] + _nn(p.astype(BF16), ck)

    cks = [c_refs[i][...].astype(BF16) for i in range(P)]
    s_list = [(_nt(ql, cks[i]) + _nn(qr, r_refs[i][...].astype(BF16))) * scale for i in range(P)]
    alpha, p_list = _softmax_step(s_list, m_scr, l_scr)
    acc = alpha * acc_scr[...]
    for i in range(P):
        acc = acc + _nn(p_list[i].astype(BF16), cks[i])
    acc_scr[...] = acc

    @pl.when(c == pl.num_programs(1) - 1)
    def _():
        o_ref[...] = (acc_scr[...] / l_scr[...]).astype(o_ref.dtype)


def mla_sample(q_lat, q_cat, new, cache_ckv, cache_kpe, layer, page_table, *, n_heads, P):
    B, rows, kv_lora = q_lat.shape
    rope_dim = cache_kpe.shape[2]
    n_pages = page_table.shape[1]
    P = _fit(n_pages, P)
    body = functools.partial(_mla_sample_body, P=P, kv_lora=kv_lora, rope_dim=rope_dim, n_heads=n_heads,
                             scale=(HEAD_DIM + rope_dim) ** -0.5)
    bspec = lambda r, w: pl.BlockSpec((None, r, w), lambda b, c, pt: (b, 0, 0))
    grid_spec = pltpu.PrefetchScalarGridSpec(
        num_scalar_prefetch=1,
        grid=(B, n_pages // P),
        in_specs=[bspec(rows, kv_lora), bspec(rows, 2 * HEAD_DIM), bspec(LANES, new.shape[-1])]
        + _page_specs(layer, n_pages, P, (LANES, kv_lora), False)
        + _page_specs(layer, n_pages, P, (rope_dim, LANES), False),
        out_specs=bspec(rows, kv_lora),
        scratch_shapes=[pltpu.VMEM((rows, 1), F32), pltpu.VMEM((rows, 1), F32), pltpu.VMEM((rows, kv_lora), F32)],
    )
    return pl.pallas_call(
        body,
        grid_spec=grid_spec,
        out_shape=jax.ShapeDtypeStruct((B, rows, kv_lora), BF16),
        compiler_params=_params("parallel", "arbitrary"),
        name="mla_sample",
    )(page_table, q_lat, q_cat, new, *([cache_ckv] * P), *([cache_kpe] * P))


def _top_blocks(gate, n_valid, n_sel):
    lane = lax.broadcasted_iota(jnp.int32, gate.shape, 1)
    g = jnp.where(lane < n_valid, gate, NEG)
    lane = lane.astype(F32)
    sel = jnp.zeros(gate.shape, F32)
    for r in range(n_sel):
        m = jnp.max(g, axis=1, keepdims=True)
        idx = jnp.min(jnp.where(g == m, lane, float(LANES)), axis=1, keepdims=True)
        pick = lane == idx
        sel = jnp.where(jnp.logical_and(pick, n_valid > r), 1.0, sel)
        g = jnp.where(pick, -jnp.inf, g)
    return sel


def _sel_column(sel, n):
    lane = lax.broadcasted_iota(jnp.int32, sel.shape, 1)
    return jnp.sum(jnp.where(lane == n, sel, 0.0), axis=1, keepdims=True) > 0.0


def _moba_prompt_body(q_ref, k_ref, v_ref, o_ref, km_scr, qb_scr, sel_scr, m_scr, l_scr, acc_scr, *, tq, n_group,
                      n_blocks, scale):
    qt = pl.program_id(2)

    @pl.when(qt == 0)
    def _():
        km_scr[...] = jnp.zeros_like(km_scr)
        for n in range(n_blocks):
            blk = k_ref[n * MOBA_BLOCK:(n + 1) * MOBA_BLOCK, :]
            km_scr[n:n + 1, :] = jnp.sum(blk, axis=0, keepdims=True) * (1.0 / MOBA_BLOCK)

    own = _idiv(qt * tq, MOBA_BLOCK)
    km = km_scr[...]
    for g in range(n_group):
        q32 = q_ref[:, g * HEAD_DIM:(g + 1) * HEAD_DIM]
        qb_scr[g] = q32.astype(BF16)
        sel_scr[g] = _top_blocks(_nt3(q32, km), own, MOBA_TOPK).astype(BF16)
    _softmax_init(m_scr, l_scr, acc_scr)

    def tile(n, causal):
        start = pl.multiple_of(n * MOBA_BLOCK, MOBA_BLOCK)
        k = k_ref[pl.ds(start, MOBA_BLOCK), :].astype(BF16)
        v = v_ref[pl.ds(start, MOBA_BLOCK), :].astype(BF16)
        if causal is None:
            j = lax.broadcasted_iota(jnp.int32, (LANES, MOBA_BLOCK), 0)
            spread = jnp.where(j == n, 1.0, 0.0).astype(BF16)
        heads = range(n_group)
        raw = [_nt(qb_scr[g], k) for g in heads]
        valid = [causal if causal is not None else _nn(sel_scr[g], spread) > 0.5 for g in heads]
        s = [jnp.where(valid[g], raw[g] * scale, NEG) for g in heads]
        m_old = [m_scr[g] for g in heads]
        m_new = [jnp.maximum(m_old[g], jnp.max(s[g], axis=1, keepdims=True)) for g in heads]
        alpha = [jnp.exp(m_old[g] - m_new[g]) for g in heads]
        p = [jnp.exp(s[g] - m_new[g]) for g in heads]
        pv = [_nn(p[g].astype(BF16), v) for g in heads]
        for g in heads:
            l_scr[g] = alpha[g] * l_scr[g] + jnp.sum(p[g], axis=1, keepdims=True)
            m_scr[g] = m_new[g]
            acc_scr[g] = alpha[g] * acc_scr[g] + pv[g]

    def body(n, _):
        tile(n, None)
        return 0

    lax.fori_loop(0, own, body, 0)
    r = lax.broadcasted_iota(jnp.int32, (tq, MOBA_BLOCK), 0)
    c = lax.broadcasted_iota(jnp.int32, (tq, MOBA_BLOCK), 1)
    tile(own, own * MOBA_BLOCK + c <= qt * tq + r)
    for g in range(n_group):
        o_ref[:, g * HEAD_DIM:(g + 1) * HEAD_DIM] = (acc_scr[g] / l_scr[g]).astype(o_ref.dtype)


def moba_prompt(qkv, *, B, T, n_heads, n_kv, tq=128):
    n_group = n_heads // n_kv
    assert MOBA_BLOCK % tq == 0 and T % MOBA_BLOCK == 0 and T // MOBA_BLOCK <= LANES
    nqt = T // tq
    gw = n_group * HEAD_DIM
    rows = n_group * tq
    body = functools.partial(_moba_prompt_body, tq=tq, n_group=n_group, n_blocks=T // MOBA_BLOCK,
                             scale=HEAD_DIM ** -0.5)
    return pl.pallas_call(
        body,
        grid=(B, n_kv, nqt),
        in_specs=[pl.BlockSpec((tq, gw), lambda b, kh, qt: (b * nqt + qt, kh)),
                  pl.BlockSpec((T, HEAD_DIM), lambda b, kh, qt: (b, n_heads + kh)),
                  pl.BlockSpec((T, HEAD_DIM), lambda b, kh, qt: (b, n_heads + n_kv + kh))],
        out_specs=pl.BlockSpec((tq, gw), lambda b, kh, qt: (b * nqt + qt, kh)),
        out_shape=jax.ShapeDtypeStruct((B * T, n_heads * HEAD_DIM), BF16),
        scratch_shapes=[pltpu.VMEM((LANES, HEAD_DIM), F32), pltpu.VMEM((n_group, tq, HEAD_DIM), BF16),
                        pltpu.VMEM((n_group, tq, LANES), BF16), pltpu.VMEM((n_group, tq, 1), F32),
                        pltpu.VMEM((n_group, tq, 1), F32), pltpu.VMEM((n_group, tq, HEAD_DIM), F32)],
        compiler_params=_params("arbitrary", "arbitrary", "arbitrary"),
        name="moba_prompt",
    )(qkv, qkv, qkv)


def _moba_select_body(pt_ref, q_ref, *rest, P, n_kv, n_past_blocks):
    k_refs = rest[:P]
    sel_ref, km_scr = rest[P], rest[P + 1]
    c = pl.program_id(1)
    pages_per_block = MOBA_BLOCK // LANES

    @pl.when(c == 0)
    def _():
        km_scr[...] = jnp.zeros_like(km_scr)

    row = lax.broadcasted_iota(jnp.int32, km_scr.shape, 0)
    km = km_scr[...]
    for i in range(P // pages_per_block):
        means = []
        for h in range(n_kv):
            tot = None
            for j in range(pages_per_block):
                page = k_refs[pages_per_block * i + j][pl.ds(h, LANES, stride=n_kv), :]
                tot = page if tot is None else tot + page
            means.append(jnp.sum(tot, axis=0, keepdims=True) * (1.0 / MOBA_BLOCK))
        mean = jnp.concatenate(means, axis=1)
        km = jnp.where(row == c * (P // pages_per_block) + i, mean, km)
    km_scr[...] = km

    @pl.when(c == pl.num_programs(1) - 1)
    def _():
        q = q_ref[...]
        rk = q.shape[0] // n_kv
        gate = jnp.concatenate([_nt3(q[h * rk:(h + 1) * rk], km[:, h * HEAD_DIM:(h + 1) * HEAD_DIM])
                                for h in range(n_kv)], axis=0)
        sel_ref[...] = _top_blocks(gate, n_past_blocks, MOBA_TOPK)


def moba_select(q, cache_k, layer, page_table, *, n_kv, P):
    B, rows, _ = q.shape
    n_pages = page_table.shape[1]
    P = _fit(n_pages, P)
    kvw = n_kv * HEAD_DIM
    n_past_blocks = n_pages * LANES // MOBA_BLOCK
    assert n_past_blocks <= LANES and P % (MOBA_BLOCK // LANES) == 0
    body = functools.partial(_moba_select_body, P=P, n_kv=n_kv, n_past_blocks=n_past_blocks)
    grid_spec = pltpu.PrefetchScalarGridSpec(
        num_scalar_prefetch=1,
        grid=(B, n_pages // P),
        in_specs=[pl.BlockSpec((None, rows, HEAD_DIM), lambda b, c, pt: (b, 0, 0))]
        + _page_specs(layer, n_pages, P, cache_k.shape[2:], False),
        out_specs=pl.BlockSpec((None, rows, LANES), lambda b, c, pt: (b, 0, 0)),
        scratch_shapes=[pltpu.VMEM((LANES, kvw), F32)],
    )
    return pl.pallas_call(
        body,
        grid_spec=grid_spec,
        out_shape=jax.ShapeDtypeStruct((B, rows, LANES), F32),
        compiler_params=_params("parallel", "arbitrary"),
        name="moba_select",
    )(page_table, q, *([cache_k] * P))


def _moba_sample_body(pt_ref, q_ref, sel_ref, kn_ref, vn_ref, *rest, P, n_kv, t_new, scale):
    k_refs, v_refs = rest[:P], rest[P:2 * P]
    o_ref, m_scr, l_scr, acc_scr = rest[2 * P:2 * P + 4]
    c = pl.program_id(1)
    q = q_ref[...].astype(BF16)
    rk = q.shape[0] // n_kv
    pages_per_block = MOBA_BLOCK // LANES

    def scores(ks):
        return jnp.concatenate([_nt(q[h * rk:(h + 1) * rk], ks[h]) for h in range(n_kv)], axis=0) * scale

    def pv(acc, p, vs):
        pb = p.astype(BF16)
        return acc + jnp.concatenate([_nn(pb[h * rk:(h + 1) * rk], vs[h]) for h in range(n_kv)], axis=0)

    @pl.when(c == 0)
    def _():
        _softmax_init(m_scr, l_scr, acc_scr)
        s = scores(_lane_heads(kn_ref[...], n_kv))
        r = lax.broadcasted_iota(jnp.int32, s.shape, 0)
        col = lax.broadcasted_iota(jnp.int32, s.shape, 1)
        s = jnp.where(col <= _imod(r, t_new), s, NEG)
        alpha, (p,) = _softmax_step([s], m_scr, l_scr)
        acc_scr[...] = pv(alpha * acc_scr[...], p, _lane_heads(vn_ref[...], n_kv))

    sel = sel_ref[...]
    s_list = []
    for i in range(P):
        valid = _sel_column(sel, _idiv(c * P + i, pages_per_block))
        s_list.append(jnp.where(valid, scores(_page_heads(k_refs[i], n_kv)), NEG))
    alpha, p_list = _softmax_step(s_list, m_scr, l_scr)
    acc = alpha * acc_scr[...]
    for i in range(P):
        acc = pv(acc, p_list[i], _page_heads(v_refs[i], n_kv))
    acc_scr[...] = acc

    @pl.when(c == pl.num_programs(1) - 1)
    def _():
        o_ref[...] = (acc_scr[...] / l_scr[...]).astype(o_ref.dtype)


def moba_sample(q, sel, k_new, v_new, cache_k, cache_v, layer, page_table, *, n_kv, t_new, P):
    B, rows, _ = q.shape
    n_pages = page_table.shape[1]
    P = _fit(n_pages, P)
    kvw = n_kv * HEAD_DIM
    body = functools.partial(_moba_sample_body, P=P, n_kv=n_kv, t_new=t_new, scale=HEAD_DIM ** -0.5)
    bspec = lambda r, w: pl.BlockSpec((None, r, w), lambda b, c, pt: (b, 0, 0))
    grid_spec = pltpu.PrefetchScalarGridSpec(
        num_scalar_prefetch=1,
        grid=(B, n_pages // P),
        in_specs=[bspec(rows, HEAD_DIM), bspec(rows, LANES), bspec(LANES, kvw), bspec(LANES, kvw)]
        + _page_specs(layer, n_pages, P, cache_k.shape[2:], False)
        + _page_specs(layer, n_pages, P, cache_v.shape[2:], False),
        out_specs=bspec(rows, HEAD_DIM),
        scratch_shapes=[pltpu.VMEM((rows, 1), F32), pltpu.VMEM((rows, 1), F32), pltpu.VMEM((rows, HEAD_DIM), F32)],
    )
    return pl.pallas_call(
        body,
        grid_spec=grid_spec,
        out_shape=jax.ShapeDtypeStruct((B, rows, HEAD_DIM), BF16),
        compiler_params=_params("parallel", "arbitrary"),
        name="moba_sample",
    )(page_table, q, sel, k_new, v_new, *([cache_k] * P), *([cache_v] * P))


def _heads_first(x, B, t_new, n_heads, width):
    return x.reshape(t_new, B, n_heads, width).transpose(1, 2, 0, 3).reshape(B, n_heads * t_new, width)


def _heads_last(o, B, t_new, n_heads, width):
    return o.reshape(B, n_heads, t_new, width).transpose(2, 0, 1, 3).reshape(t_new * B, n_heads * width)


def _batch_major(x, B, t_new):
    return x.reshape(t_new, B, x.shape[-1]).transpose(1, 0, 2)


def _time_major(x, B, t_new):
    k = x.shape[1] // t_new
    return x.reshape(B, t_new, k * x.shape[-1]).transpose(1, 0, 2).reshape(t_new * B, k * x.shape[-1])


def _pad_new(x_bt):
    return jnp.pad(x_bt, ((0, 0), (0, LANES - x_bt.shape[1]), (0, 0)))


TM_PROMPT = 1024
TM_FFN = 512
TN = 512
PAGES_PER_STEP = 8


def kernel(x_prompt, x_sample, cache_sb_k, cache_sb_v, cache_mla_ckv, cache_mla_kpe, cache_moba_k, cache_moba_v, page_table, c_prompt, c_sample, mod_w, mod_b, norm_mix, norm_ffn, final_norm, sb_wq, sb_wk, sb_wv, sb_wo, mla_wdq, mla_q_norm, mla_wuq, mla_wdkv, mla_kv_norm, mla_wuk, mla_wuv, mla_wo, moba_wq, moba_wk, moba_wv, moba_wo, ffn_wg, ffn_wu, ffn_wd, moe_wr, moe_br, moe_wg, moe_wu, moe_wd):
    Bp, T, D = x_prompt.shape
    Bs, Ts, _ = x_sample.shape
    depth = mod_w.shape[0]
    n_pages, page = page_table.shape[1], cache_sb_k.shape[2]
    past_len = n_pages * page
    assert page == LANES and Ts == SUBLANES and past_len % MOBA_BLOCK == 0
    n_heads = sb_wq.shape[2] // HEAD_DIM
    sb_kv = sb_wk.shape[2] // HEAD_DIM
    moba_kv = moba_wk.shape[2] // HEAD_DIM
    q_lora, kv_lora = mla_wdq.shape[2], mla_kv_norm.shape[1]
    rope_dim = mla_wdkv.shape[2] - kv_lora
    nope = mla_wuk.shape[3]
    assert nope == HEAD_DIM and mla_wuv.shape[3] == HEAD_DIM and rope_dim == HEAD_DIM // 2
    pos_p = jnp.arange(T, dtype=jnp.int32)
    pos_s = past_len + jnp.arange(Ts, dtype=jnp.int32)
    streams = ((Bp, T, Bp, T, pos_p, TM_PROMPT), (Ts, Bs, Bs, Ts, jnp.repeat(pos_s, Bs), Ts * Bs))

    n_c = Bs + Bp
    c_all = jnp.concatenate([c_sample, c_prompt, jnp.zeros((-n_c % SUBLANES, D), F32)], axis=0)
    mod_all = modulation_all(c_all, mod_w, mod_b).reshape(depth, c_all.shape[0], 6, D)

    flat_pages = lambda c: c.reshape(c.shape[0], c.shape[1], c.shape[2] * c.shape[3], c.shape[4])
    sb_cache_k, sb_cache_v = flat_pages(cache_sb_k), flat_pages(cache_sb_v)
    mb_cache_k, mb_cache_v = flat_pages(cache_moba_k), flat_pages(cache_moba_v)
    cache_kpe_t = cache_mla_kpe.transpose(0, 1, 3, 2)

    xs = [x_prompt, x_sample.transpose(1, 0, 2)]
    outs = {k: ([], []) for k in ("sb_k", "sb_v", "mla_c", "mla_r", "mb_k", "mb_v")}
    for i in range(depth):
        kind, j, f = i % N_MIXERS, i // N_MIXERS, i // 2
        mods = (mod_all[i, Bs:Bs + Bp].transpose(1, 0, 2).reshape(6, Bp, 1, D),
                mod_all[i, :Bs].transpose(1, 0, 2).reshape(6, 1, Bs, D))
        for s, (G, R, B, Tn, pos, tm) in enumerate(streams):
            x3, mod = xs[s], mods[s]
            is_sample = s == 1
            to_bt = (lambda y: _batch_major(y, B, Tn)) if is_sample else (lambda y: y.reshape(B, Tn, y.shape[-1]))
            if kind == 0:
                w = jnp.concatenate([sb_wq[j], sb_wk[j], sb_wv[j]], axis=1)
                qkv = ada_linear(x3, mod, 0, 1, norm_mix[i], w, tm=tm, tn=TN, name="sb_qkv")
                qw, kw = n_heads * HEAD_DIM, sb_kv * HEAD_DIM
                k_new, v_new = to_bt(qkv[:, qw:qw + kw]), to_bt(qkv[:, qw + kw:])
                outs["sb_k"][s].append(k_new.reshape(B, Tn, sb_kv, HEAD_DIM))
                outs["sb_v"][s].append(v_new.reshape(B, Tn, sb_kv, HEAD_DIM))
                if is_sample:
                    q = _heads_first(qkv[:, :qw], B, Tn, n_heads, HEAD_DIM)
                    o = sb_sample(q, _pad_new(k_new), _pad_new(v_new), sb_cache_k, sb_cache_v, j,
                                  page_table, n_kv=sb_kv, t_new=Tn, P=PAGES_PER_STEP)
                    o = _heads_last(o, B, Tn, n_heads, HEAD_DIM)
                else:
                    o = sb_prompt(qkv, B=B, T=Tn, n_heads=n_heads, n_kv=sb_kv)
                wo = sb_wo[j]
            elif kind == 1:
                half = rope_dim // 2
                c64, s1, s2 = _rope_tables(pos, rope_dim)
                w = jnp.concatenate([mla_wdq[j], mla_wdkv[j], jnp.zeros((D, LANES - rope_dim), F32)], axis=1)
                tmd = min(tm, TM_FFN)
                tabs, tab_specs = _row_tables((c64, s1, s2), G, R, tmd)
                down = ada_linear(
                    x3, mod, 0, 1, norm_mix[i], w, tm=tmd, tn=w.shape[1],
                    epilogue=functools.partial(_epi_mla_down, q_lora=q_lora, kv_lora=kv_lora, half=half),
                    extra=(mla_q_norm[j].reshape(1, q_lora), mla_kv_norm[j].reshape(1, kv_lora)) + tabs,
                    extra_specs=[_const_spec(q_lora), _const_spec(kv_lora)] + tab_specs,
                    name="mla_down")
                ckv = to_bt(down[:, q_lora:q_lora + kv_lora])
                kpe = to_bt(down[:, q_lora + kv_lora:q_lora + kv_lora + rope_dim])
                outs["mla_c"][s].append(ckv)
                outs["mla_r"][s].append(kpe)
                wuq = mla_wuq[j].reshape(q_lora, n_heads, nope + rope_dim)
                wuq = jnp.pad(wuq, ((0, 0), (0, 0), (0, 2 * HEAD_DIM - nope - rope_dim))).reshape(q_lora, -1)
                gb, rb, _ = _row_tiling(G, R, tm)
                tabs, tab_specs = _row_tables((c64, s1, s2), G, R, tm)
                q_cat = linear(down, wuq, x_cols=(q_lora, 0), tm=gb * rb, tn=TN, out_dtype=BF16,
                               epilogue=functools.partial(_epi_q_up, half=half), extra=tabs, extra_specs=tab_specs,
                               name="mla_q_up")
                wuk2 = mla_wuk[j].reshape(kv_lora, n_heads * nope)
                wuv2 = mla_wuv[j].reshape(kv_lora, n_heads * HEAD_DIM)
                if is_sample:
                    q_lat = head_linear(q_cat, wuk2, n_heads=n_heads, x_width=HEAD_DIM, x_stride=2, x_off=0,
                                        w_width=HEAD_DIM, transpose_w=True, out_width=kv_lora, tm=tm, name="mla_q_lat")
                    per_batch = lambda y, wd: _batch_major(y, B, Tn).reshape(B, Tn * n_heads, wd)
                    new = _pad_new(_batch_major(down[:, q_lora:], B, Tn))
                    o_lat = mla_sample(per_batch(q_lat, kv_lora), per_batch(q_cat, 2 * HEAD_DIM), new,
                                       cache_mla_ckv, cache_kpe_t, j, page_table, n_heads=n_heads,
                                       P=PAGES_PER_STEP)
                    o = head_linear(_time_major(o_lat, B, Tn), wuv2, n_heads=n_heads, x_width=kv_lora,
                                    x_stride=1, x_off=0, w_width=HEAD_DIM, transpose_w=False, out_width=HEAD_DIM,
                                    tm=tm, name="mla_o_up")
                else:
                    assert q_lora % kv_lora == 0
                    kv_up = linear(down, jnp.concatenate([wuk2, wuv2], axis=1), x_cols=(kv_lora, q_lora // kv_lora),
                                   tm=tm, tn=TN, out_dtype=BF16, name="mla_kv_up")
                    o = mla_prompt(q_cat, kv_up, down, B=B, T=Tn, n_heads=n_heads,
                                   kr_block=(q_lora + kv_lora) // LANES)
                wo = mla_wo[j]
            else:
                c128, s1, s2 = _rope_tables(pos, HEAD_DIM)
                tabs, tab_specs = _row_tables((c128, s1 + s2), G, R, tm)
                w = jnp.concatenate([moba_wq[j], moba_wk[j], moba_wv[j]], axis=1)
                qw, kw = n_heads * HEAD_DIM, moba_kv * HEAD_DIM
                tn = 2 * LANES
                qkv = ada_linear(x3, mod, 0, 1, norm_mix[i], w, tm=tm, tn=tn,
                                 epilogue=functools.partial(_epi_rope128, n_rope_tiles=(qw + kw) // tn),
                                 extra=tabs, extra_specs=tab_specs, name="moba_qkv")
                k_new, v_new = to_bt(qkv[:, qw:qw + kw]), to_bt(qkv[:, qw + kw:])
                outs["mb_k"][s].append(k_new.reshape(B, Tn, moba_kv, HEAD_DIM))
                outs["mb_v"][s].append(v_new.reshape(B, Tn, moba_kv, HEAD_DIM))
                if is_sample:
                    q = _heads_first(qkv[:, :qw], B, Tn, n_heads, HEAD_DIM)
                    sel = moba_select(q, mb_cache_k, j, page_table, n_kv=moba_kv, P=PAGES_PER_STEP)
                    o = moba_sample(q, sel, _pad_new(k_new), _pad_new(v_new), mb_cache_k, mb_cache_v, j,
                                    page_table, n_kv=moba_kv, t_new=Tn, P=PAGES_PER_STEP)
                    o = _heads_last(o, B, Tn, n_heads, HEAD_DIM)
                else:
                    o = moba_prompt(qkv, B=B, T=Tn, n_heads=n_heads, n_kv=moba_kv)
                wo = moba_wo[j]
            x3 = linear_res(o, wo, x3, mod, 2, tm=tm, tn=TN, name="mixer_out")
            tmf = min(tm, TM_FFN)
            if i % 2 == 0:
                x3 = ffn_dense(x3, mod, norm_ffn[i], ffn_wg, ffn_wu, ffn_wd, f, tm=tmf, tf=256)
            else:
                x3 = moe_dense(x3, mod, norm_ffn[i], moe_wr[f], moe_br[f], moe_wg, moe_wu, moe_wd, f,
                               tm=tmf, tf=256)
            xs[s] = x3
    y_prompt = rms_final(xs[0], final_norm, tm=TM_FFN)
    y_sample = rms_final(xs[1], final_norm, tm=TM_FFN).transpose(1, 0, 2)
    st = lambda key, s: jnp.stack(outs[key][s])
    return (y_prompt, y_sample,
            st("sb_k", 0), st("sb_v", 0), st("sb_k", 1), st("sb_v", 1),
            st("mla_c", 0), st("mla_r", 0), st("mla_c", 1), st("mla_r", 1),
            st("mb_k", 0), st("mb_v", 0), st("mb_k", 1), st("mb_v", 1))
```

```python
import functools

import jax
import jax.numpy as jnp
from jax import lax
from jax.experimental import pallas as pl
from jax.experimental.pallas import tpu as pltpu

F32 = jnp.float32
BF16 = jnp.bfloat16

LANES = 128
SUBLANES = 8
HEAD_DIM = 128
EPS = 1e-6
NEG = -1e30
ROPE_THETA = 10000.0
MOBA_BLOCK = 256
MOBA_TOPK = 3
N_MIXERS = 3
SB_DEAD = -104.0
VMEM_LIMIT = 56 << 20


def _params(*sem):
    return pltpu.CompilerParams(dimension_semantics=sem, vmem_limit_bytes=VMEM_LIMIT)


def _nt(a, b):
    return lax.dot_general(a, b, (((1,), (1,)), ((), ())), preferred_element_type=F32)


def _nn(a, b):
    return jnp.dot(a, b, preferred_element_type=F32)


def _split(x):
    hi = x.astype(BF16)
    lo = (x - hi.astype(F32)).astype(BF16)
    return hi, lo


def _nt3(a, b):
    ah, al = _split(a)
    bh, bl = _split(b)
    return _nt(ah, bh) + _nt(ah, bl) + _nt(al, bh)


def _softplus(z):
    return jnp.maximum(z, 0.0) + jnp.log(1.0 + jnp.exp(-jnp.abs(z)))


def _silu(g):
    return g * (1.0 / (1.0 + jnp.exp(-g)))


def _rms(x, g):
    return x * lax.rsqrt(jnp.mean(x * x, axis=-1, keepdims=True) + EPS) * g


def _row_tiling(G, R, tm):
    if R >= tm:
        assert R % tm == 0
        return 1, tm, R // tm
    assert R % SUBLANES == 0
    gb = max(min(tm // R, G), 1)
    while G % gb:
        gb -= 1
    return gb, R, 1


def _log2(n):
    assert n > 0 and n & (n - 1) == 0, n
    return n.bit_length() - 1


def _imod(x, n):
    _log2(n)
    return x & (n - 1)


def _idiv(x, n):
    return x >> _log2(n)


def _mod_spec(mod, j, gb, rb, nrb, width, col=None):
    _, gm, rm, _ = mod.shape
    assert gm == 1 or rm == 1
    block = (None, gb if gm > 1 else 1, rb if rm > 1 else 1, width)

    def imap(*ids):
        i = ids[0]
        return (j, i // nrb if gm > 1 else 0, i % nrb if rm > 1 else 0, 0 if col is None else ids[col])

    return pl.BlockSpec(block, imap)


def _fit(n, want):
    t = min(n, want)
    while n % t:
        t -= LANES if t > LANES else 1
    return t


def _rope_tables(pos, dim):
    half = dim // 2
    inv_freq = ROPE_THETA ** (-jnp.arange(half, dtype=F32) / half)
    ang = pos.astype(F32)[:, None] * inv_freq[None, :]
    cos, sin = jnp.cos(ang), jnp.sin(ang)
    z = jnp.zeros((pos.shape[0], LANES - dim), F32)
    zh = jnp.zeros_like(cos)
    c = jnp.concatenate([cos, cos, z], axis=1)
    s1 = jnp.concatenate([-sin, zh, z], axis=1)
    s2 = jnp.concatenate([zh, sin, z], axis=1)
    return c, s1, s2


def _rope128(a, c, s):
    return a * c + pltpu.roll(a, HEAD_DIM // 2, axis=1) * s


def _rope64(a, c, s1, s2, half):
    return a * c + pltpu.roll(a, LANES - half, axis=1) * s1 + pltpu.roll(a, half, axis=1) * s2


def _mod_body(c_ref, w_ref, b_ref, o_ref):
    a = _silu(c_ref[...]).astype(BF16)
    o_ref[...] = _nn(a, w_ref[...].astype(BF16)) + b_ref[...]


def modulation_all(c_all, mod_w, mod_b):
    L, D, N = mod_w.shape
    Bp = c_all.shape[0]
    tn = _fit(N, 1024)
    return pl.pallas_call(
        _mod_body,
        grid=(L, N // tn),
        in_specs=[pl.BlockSpec((Bp, D), lambda l, n: (0, 0)),
                  pl.BlockSpec((None, D, tn), lambda l, n: (l, 0, n)),
                  pl.BlockSpec((None, 1, tn), lambda l, n: (l, 0, n))],
        out_specs=pl.BlockSpec((None, Bp, tn), lambda l, n: (l, 0, n)),
        out_shape=jax.ShapeDtypeStruct((L, Bp, N), F32),
        compiler_params=_params("parallel", "parallel"),
        name="modulation",
    )(c_all, mod_w, mod_b.reshape(L, 1, N))


def _ada_h(x_ref, sh_ref, sc_ref, g_ref):
    x = x_ref[...]
    h = _rms(x, g_ref[...]) * (1.0 + sc_ref[...]) + sh_ref[...]
    return h.reshape(h.shape[0] * h.shape[1], h.shape[2])


def _ada_linear_body(x_ref, sh_ref, sc_ref, g_ref, w_ref, *rest, epilogue, n_extra):
    extra, o_ref, h_scr = rest[:n_extra], rest[n_extra], rest[n_extra + 1]
    n = pl.program_id(1)

    @pl.when(n == 0)
    def _():
        h_scr[...] = _ada_h(x_ref, sh_ref, sc_ref, g_ref).astype(BF16)

    acc = _nn(h_scr[...], w_ref[...].astype(BF16))
    epilogue(acc, n, extra, o_ref)


def _epi_plain(acc, n, extra, o_ref):
    o_ref[...] = acc.astype(o_ref.dtype)


def _epi_rope128(acc, n, extra, o_ref, *, n_rope_tiles):
    c_ref, s_ref = extra

    @pl.when(n < n_rope_tiles)
    def _():
        c, s = c_ref[...], s_ref[...]
        for j in range(acc.shape[1] // LANES):
            sl = slice(j * LANES, (j + 1) * LANES)
            o_ref[:, sl] = _rope128(acc[:, sl], c, s).astype(o_ref.dtype)

    @pl.when(n >= n_rope_tiles)
    def _():
        o_ref[...] = acc.astype(o_ref.dtype)


def _epi_mla_down(acc, n, extra, o_ref, *, q_lora, kv_lora, half):
    qn_ref, kn_ref, c_ref, s1_ref, s2_ref = extra
    o_ref[:, 0:q_lora] = _rms(acc[:, 0:q_lora], qn_ref[...])
    o_ref[:, q_lora:q_lora + kv_lora] = _rms(acc[:, q_lora:q_lora + kv_lora], kn_ref[...])
    r0 = q_lora + kv_lora
    o_ref[:, r0:r0 + LANES] = _rope64(acc[:, r0:r0 + LANES], c_ref[...], s1_ref[...], s2_ref[...], half)


def ada_linear(x3, mod, j_shift, j_scale, gain, w, *, tm, tn, epilogue=_epi_plain, extra=(), extra_specs=(),
               out_dtype=F32, name):
    G, R, K = x3.shape
    N = w.shape[1]
    gb, rb, nrb = _row_tiling(G, R, tm)
    tm = gb * rb
    tn = _fit(N, tn)
    body = functools.partial(_ada_linear_body, epilogue=epilogue, n_extra=len(extra))
    mspec = lambda j: _mod_spec(mod, j, gb, rb, nrb, K)
    return pl.pallas_call(
        body,
        grid=(G * R // tm, N // tn),
        in_specs=[pl.BlockSpec((gb, rb, K), lambda i, n: (i // nrb, i % nrb, 0)),
                  mspec(j_shift), mspec(j_scale),
                  pl.BlockSpec((1, K), lambda i, n: (0, 0)),
                  pl.BlockSpec((K, tn), lambda i, n: (0, n))] + list(extra_specs),
        out_specs=pl.BlockSpec((tm, tn), lambda i, n: (i, n)),
        out_shape=jax.ShapeDtypeStruct((G * R, N), out_dtype),
        scratch_shapes=[pltpu.VMEM((tm, K), BF16)],
        compiler_params=_params("parallel", "arbitrary"),
        name=name,
    )(x3, mod, mod, gain.reshape(1, K), w, *extra)


def _row_tables(tabs, G, R, tm):
    gb, rb, _ = _row_tiling(G, R, tm)
    tm = gb * rb
    n = tabs[0].shape[0]
    if n < tm:
        assert tm % n == 0
        tabs = tuple(jnp.tile(t, (tm // n, 1)) for t in tabs)
        n = tm
    assert n % tm == 0
    return tabs, [pl.BlockSpec((tm, LANES), lambda i, c: (i % (n // tm), 0)) for _ in tabs]


def _const_spec(width):
    return pl.BlockSpec((1, width), lambda i, n: (0, 0))


def _linear_body(x_ref, w_ref, *rest, epilogue, n_extra, transpose_w):
    extra, o_ref = rest[:n_extra], rest[n_extra]
    x = x_ref[...].astype(BF16)
    w = w_ref[...].astype(BF16)
    acc = _nt(x, w) if transpose_w else _nn(x, w)
    epilogue(acc, pl.program_id(1), extra, o_ref)


def _epi_q_up(acc, n, extra, o_ref, *, half):
    c_ref, s1_ref, s2_ref = extra
    for j in range(acc.shape[1] // LANES):
        sl = slice(j * LANES, (j + 1) * LANES)
        a = acc[:, sl]
        if j % 2:
            a = _rope64(a, c_ref[...], s1_ref[...], s2_ref[...], half)
        o_ref[:, sl] = a.astype(o_ref.dtype)


def linear(x, w, *, x_cols, tm, tn, out_dtype, epilogue=_epi_plain, extra=(), extra_specs=(), name):
    M = x.shape[0]
    kw, kb = x_cols
    N = w.shape[1]
    tm, tn = _fit(M, tm), _fit(N, tn)
    body = functools.partial(_linear_body, epilogue=epilogue, n_extra=len(extra), transpose_w=False)
    return pl.pallas_call(
        body,
        grid=(M // tm, N // tn),
        in_specs=[pl.BlockSpec((tm, kw), lambda i, n: (i, kb)),
                  pl.BlockSpec((kw, tn), lambda i, n: (0, n))] + list(extra_specs),
        out_specs=pl.BlockSpec((tm, tn), lambda i, n: (i, n)),
        out_shape=jax.ShapeDtypeStruct((M, N), out_dtype),
        compiler_params=_params("parallel", "arbitrary"),
        name=name,
    )(x, w, *extra)


def head_linear(x, w, *, n_heads, x_width, x_stride, x_off, w_width, transpose_w, out_width, tm, name):
    M = x.shape[0]
    tm = _fit(M, tm)
    body = functools.partial(_linear_body, epilogue=_epi_plain, n_extra=0, transpose_w=transpose_w)
    return pl.pallas_call(
        body,
        grid=(M // tm, n_heads),
        in_specs=[pl.BlockSpec((tm, x_width), lambda i, h: (i, h * x_stride + x_off)),
                  pl.BlockSpec((w.shape[0], w_width), lambda i, h: (0, h))],
        out_specs=pl.BlockSpec((tm, out_width), lambda i, h: (i, h)),
        out_shape=jax.ShapeDtypeStruct((M, n_heads * out_width), BF16),
        compiler_params=_params("parallel", "arbitrary"),
        name=name,
    )(x, w)


def _linear_res_body(a_ref, w_ref, x_ref, gate_ref, o_ref):
    acc = _nn(a_ref[...].astype(BF16), w_ref[...].astype(BF16))
    o_ref[...] = x_ref[...] + gate_ref[...] * acc.reshape(x_ref.shape)


def linear_res(a, w, x3, mod, j_gate, *, tm, tn, name):
    G, R, N = x3.shape
    K = a.shape[1]
    gb, rb, nrb = _row_tiling(G, R, tm)
    tm = gb * rb
    tn = _fit(N, tn)
    return pl.pallas_call(
        _linear_res_body,
        grid=(G * R // tm, N // tn),
        in_specs=[pl.BlockSpec((tm, K), lambda i, n: (i, 0)),
                  pl.BlockSpec((K, tn), lambda i, n: (0, n)),
                  pl.BlockSpec((gb, rb, tn), lambda i, n: (i // nrb, i % nrb, n)),
                  _mod_spec(mod, j_gate, gb, rb, nrb, tn, col=1)],
        out_specs=pl.BlockSpec((gb, rb, tn), lambda i, n: (i // nrb, i % nrb, n)),
        out_shape=jax.ShapeDtypeStruct(x3.shape, F32),
        compiler_params=_params("parallel", "arbitrary"),
        name=name,
    )(a, w, x3, mod)


def _ffn_body(x_ref, sh_ref, sc_ref, gate_ref, g_ref, wg_ref, wu_ref, wd_ref, o_ref, h_scr, acc_scr):
    f = pl.program_id(1)

    @pl.when(f == 0)
    def _():
        h_scr[...] = _ada_h(x_ref, sh_ref, sc_ref, g_ref).astype(BF16)
        acc_scr[...] = jnp.zeros_like(acc_scr)

    h = h_scr[...]
    a = _silu(_nn(h, wg_ref[...].astype(BF16))) * _nn(h, wu_ref[...].astype(BF16))
    acc_scr[...] += _nn(a.astype(BF16), wd_ref[...].astype(BF16))

    @pl.when(f == pl.num_programs(1) - 1)
    def _():
        o_ref[...] = x_ref[...] + gate_ref[...] * acc_scr[...].reshape(x_ref.shape)


def ffn_dense(x3, mod, gain, wg, wu, wd, layer, *, tm, tf):
    G, R, D = x3.shape
    Fd = wg.shape[2]
    gb, rb, nrb = _row_tiling(G, R, tm)
    tm = gb * rb
    tf = _fit(Fd, tf)
    mspec = lambda j: _mod_spec(mod, j, gb, rb, nrb, D)
    xspec = pl.BlockSpec((gb, rb, D), lambda i, f: (i // nrb, i % nrb, 0))
    return pl.pallas_call(
        _ffn_body,
        grid=(G * R // tm, Fd // tf),
        in_specs=[xspec, mspec(3), mspec(4), mspec(5),
                  pl.BlockSpec((1, D), lambda i, f: (0, 0)),
                  pl.BlockSpec((None, D, tf), lambda i, f: (layer, 0, f)),
                  pl.BlockSpec((None, D, tf), lambda i, f: (layer, 0, f)),
                  pl.BlockSpec((None, tf, D), lambda i, f: (layer, f, 0))],
        out_specs=xspec,
        out_shape=jax.ShapeDtypeStruct(x3.shape, F32),
        scratch_shapes=[pltpu.VMEM((tm, D), BF16), pltpu.VMEM((tm, D), F32)],
        compiler_params=_params("parallel", "arbitrary"),
        name="ffn_dense",
    )(x3, mod, mod, mod, gain.reshape(1, D), wg, wu, wd)


def _route_body(x_ref, sh_ref, sc_ref, g_ref, wr_ref, br_ref, h_ref, r_ref):
    h = _ada_h(x_ref, sh_ref, sc_ref, g_ref)
    h_ref[...] = h
    hh, hl = _split(h)
    wh, wl = _split(wr_ref[...])
    logits = _nn(hh, wh) + _nn(hh, wl) + _nn(hl, wh) + br_ref[...]
    lane = lax.broadcasted_iota(jnp.int32, logits.shape, 1).astype(F32)
    m1 = jnp.max(logits, axis=1, keepdims=True)
    i1 = jnp.min(jnp.where(logits == m1, lane, float(LANES)), axis=1, keepdims=True)
    rest = jnp.where(lane == i1, -jnp.inf, logits)
    m2 = jnp.max(rest, axis=1, keepdims=True)
    i2 = jnp.min(jnp.where(rest == m2, lane, float(LANES)), axis=1, keepdims=True)
    e = jnp.exp(m2 - m1)
    den = 1.0 + e
    rec = jnp.where(lane == 0.0, i1, jnp.where(lane == 1.0, i2, jnp.where(lane == 2.0, 1.0 / den, e / den)))
    r_ref[...] = jnp.where(lane < 4.0, rec, 0.0)


def moe_route(x3, mod, gain, wr, br, *, tm):
    G, R, D = x3.shape
    E = wr.shape[1]
    gb, rb, nrb = _row_tiling(G, R, tm)
    tm = gb * rb
    wr_pad = jnp.zeros((D, LANES), F32).at[:, :E].set(wr)
    br_pad = jnp.full((1, LANES), NEG, F32).at[0, :E].set(br)
    mspec = lambda j: _mod_spec(mod, j, gb, rb, nrb, D)
    return pl.pallas_call(
        _route_body,
        grid=(G * R // tm,),
        in_specs=[pl.BlockSpec((gb, rb, D), lambda i: (i // nrb, i % nrb, 0)), mspec(3), mspec(4),
                  pl.BlockSpec((1, D), lambda i: (0, 0)),
                  pl.BlockSpec((D, LANES), lambda i: (0, 0)),
                  pl.BlockSpec((1, LANES), lambda i: (0, 0))],
        out_specs=[pl.BlockSpec((tm, D), lambda i: (i, 0)), pl.BlockSpec((tm, LANES), lambda i: (i, 0))],
        out_shape=[jax.ShapeDtypeStruct((G * R, D), F32), jax.ShapeDtypeStruct((G * R, LANES), F32)],
        compiler_params=_params("parallel"),
        name="moe_route",
    )(x3, mod, mod, gain.reshape(1, D), wr_pad, br_pad)


def _moe_plan(route, n_experts, tm):
    M = route.shape[0]
    n_tiles = -(-(2 * M + n_experts * (tm - 1)) // tm)
    expert = route[:, 0:2].astype(jnp.int32).reshape(2 * M)
    onehot = (expert[:, None] == jnp.arange(n_experts, dtype=jnp.int32)[None, :]).astype(jnp.int32)
    rank = jnp.cumsum(onehot, axis=0) - onehot
    count = jnp.sum(onehot, axis=0)
    padded = (count + tm - 1) // tm * tm
    ends = jnp.cumsum(padded)
    dest = jnp.sum(onehot * ((ends - padded)[None, :] + rank), axis=1)
    row_token = jnp.zeros((n_tiles * tm,), jnp.int32).at[dest].set(jnp.arange(2 * M, dtype=jnp.int32) // 2)
    tile_start = jnp.arange(n_tiles, dtype=jnp.int32) * tm
    n_used = (ends[-1] // tm).reshape(1)
    last_start = (n_used - 1) * tm
    tile_expert = jnp.sum((jnp.minimum(tile_start, last_start)[:, None] >= ends[None, :]).astype(jnp.int32), axis=1)
    return row_token, tile_expert, n_used, dest


def _row_copy(src_hbm, row, dst, r, sem):
    return pltpu.make_async_copy(src_hbm.at[pl.ds(row, 1), :], dst.at[pl.ds(r, 1), :], sem)


def _moe_group_body(te_ref, rt_ref, nu_ref, h_hbm, wg_ref, wu_ref, wd_ref, o_ref, xg_scr, hb_scr, acc_scr, sem, *,
                    tm):
    t, f = pl.program_id(0), pl.program_id(1)
    used = t < nu_ref[0]

    @pl.when(jnp.logical_and(used, f == 0))
    def _():
        base = t * tm

        def issue(r, _):
            _row_copy(h_hbm, rt_ref[base + r], xg_scr, r, sem).start()
            return 0

        def wait(r, _):
            _row_copy(h_hbm, 0, xg_scr, r, sem).wait()
            return 0

        lax.fori_loop(0, tm, issue, 0)
        lax.fori_loop(0, tm, wait, 0)
        hb_scr[...] = xg_scr[...].astype(BF16)
        acc_scr[...] = jnp.zeros_like(acc_scr)

    @pl.when(used)
    def _():
        h = hb_scr[...]
        a = _silu(_nn(h, wg_ref[...].astype(BF16))) * _nn(h, wu_ref[...].astype(BF16))
        acc_scr[...] += _nn(a.astype(BF16), wd_ref[...].astype(BF16))

    @pl.when(f == pl.num_programs(1) - 1)
    def _():
        o_ref[...] = jnp.where(used, acc_scr[...], 0.0)


def moe_grouped(h, row_token, tile_expert, n_used, wg, wu, wd, layer, *, tm, tf):
    M, D = h.shape
    Fe = wg.shape[3]
    tf = _fit(Fe, tf)
    n_tiles = tile_expert.shape[0]
    nf = Fe // tf

    def fblock(t, f, nu):
        return jnp.where(t < nu[0], f, nf - 1)

    grid_spec = pltpu.PrefetchScalarGridSpec(
        num_scalar_prefetch=3,
        grid=(n_tiles, nf),
        in_specs=[pl.BlockSpec(memory_space=pl.ANY),
                  pl.BlockSpec((None, None, D, tf), lambda t, f, te, rt, nu: (layer, te[t], 0, fblock(t, f, nu))),
                  pl.BlockSpec((None, None, D, tf), lambda t, f, te, rt, nu: (layer, te[t], 0, fblock(t, f, nu))),
                  pl.BlockSpec((None, None, tf, D), lambda t, f, te, rt, nu: (layer, te[t], fblock(t, f, nu), 0))],
        out_specs=pl.BlockSpec((tm, D), lambda t, f, te, rt, nu: (t, 0)),
        scratch_shapes=[pltpu.VMEM((tm, D), F32), pltpu.VMEM((tm, D), BF16), pltpu.VMEM((tm, D), F32),
                        pltpu.SemaphoreType.DMA(())],
    )
    return pl.pallas_call(
        functools.partial(_moe_group_body, tm=tm),
        grid_spec=grid_spec,
        out_shape=jax.ShapeDtypeStruct((n_tiles * tm, D), F32),
        compiler_params=_params("arbitrary", "arbitrary"),
        name="moe_grouped",
    )(tile_expert, row_token, n_used, h, wg, wu, wd)


def _moe_combine_body(pos_ref, yg_hbm, route_ref, x_ref, gate_ref, o_ref, b0_scr, b1_scr, sem, *, tm):
    base = pl.program_id(0) * tm

    def issue(r, _):
        _row_copy(yg_hbm, pos_ref[2 * (base + r)], b0_scr, r, sem).start()
        _row_copy(yg_hbm, pos_ref[2 * (base + r) + 1], b1_scr, r, sem).start()
        return 0

    def wait(r, _):
        _row_copy(yg_hbm, 0, b0_scr, r, sem).wait()
        _row_copy(yg_hbm, 0, b1_scr, r, sem).wait()
        return 0

    lax.fori_loop(0, tm, issue, 0)
    lax.fori_loop(0, tm, wait, 0)
    y = route_ref[:, 2:3] * b0_scr[...] + route_ref[:, 3:4] * b1_scr[...]
    o_ref[...] = x_ref[...] + gate_ref[...] * y.reshape(x_ref.shape)


def moe_combine(yg, pos, route, x3, mod, *, tm):
    G, R, D = x3.shape
    gb, rb, nrb = _row_tiling(G, R, tm)
    tm = gb * rb
    xspec = pl.BlockSpec((gb, rb, D), lambda i, pos: (i // nrb, i % nrb, 0))
    grid_spec = pltpu.PrefetchScalarGridSpec(
        num_scalar_prefetch=1,
        grid=(G * R // tm,),
        in_specs=[pl.BlockSpec(memory_space=pl.ANY), pl.BlockSpec((tm, LANES), lambda i, pos: (i, 0)), xspec,
                  _mod_spec(mod, 5, gb, rb, nrb, D)],
        out_specs=xspec,
        scratch_shapes=[pltpu.VMEM((tm, D), F32), pltpu.VMEM((tm, D), F32), pltpu.SemaphoreType.DMA(())],
    )
    return pl.pallas_call(
        functools.partial(_moe_combine_body, tm=tm),
        grid_spec=grid_spec,
        out_shape=jax.ShapeDtypeStruct(x3.shape, F32),
        compiler_params=_params("arbitrary"),
        name="moe_combine",
    )(pos, yg, route, x3, mod)


def moe_top2(x3, mod, gain, wr, br, wg, wu, wd, layer, *, tm_route, tm_group, tf):
    n_experts = wg.shape[1]
    h, route = moe_route(x3, mod, gain, wr, br, tm=tm_route)
    row_token, tile_expert, n_used, pos = _moe_plan(route, n_experts, tm_group)
    yg = moe_grouped(h, row_token, tile_expert, n_used, wg, wu, wd, layer, tm=tm_group, tf=tf)
    return moe_combine(yg, pos, route, x3, mod, tm=256)


def _final_norm_body(x_ref, g_ref, o_ref):
    o_ref[...] = _rms(x_ref[...], g_ref[...])


def rms_final(x3, gain, *, tm):
    G, R, D = x3.shape
    gb, rb, nrb = _row_tiling(G, R, tm)
    spec = pl.BlockSpec((gb, rb, D), lambda i: (i // nrb, i % nrb, 0))
    return pl.pallas_call(
        _final_norm_body,
        grid=(G * R // (gb * rb),),
        in_specs=[spec, pl.BlockSpec((1, D), lambda i: (0, 0))],
        out_specs=spec,
        out_shape=jax.ShapeDtypeStruct(x3.shape, F32),
        compiler_params=_params("parallel"),
        name="final_norm",
    )(x3, gain.reshape(1, D))


def _cumsum_matrix(tk):
    j = lax.broadcasted_iota(jnp.int32, (2 * tk, tk), 0)
    s = lax.broadcasted_iota(jnp.int32, (2 * tk, tk), 1)
    jj = jnp.where(j >= tk, j - tk, j)
    return jnp.where(jj > s, 1.0, 0.0).astype(BF16)


def _sb_tile(z, past, carry, u2):
    sp = _softplus(z)
    log_beta = z - sp
    log_stay = -sp
    if past is not None:
        log_stay = jnp.where(past, log_stay, 0.0)
    hi, lo = _split(log_stay)
    local = _nn(jnp.concatenate([hi, lo], axis=1), u2)
    w = jnp.exp(log_beta + local + carry)
    if past is not None:
        w = jnp.where(past, w, 0.0)
    return w, jnp.sum(log_stay, axis=1, keepdims=True)


def _sb_live(carry):
    return jnp.max(carry) > SB_DEAD


def _sb_prompt_body(q_ref, k_ref, v_ref, o_ref, qs_scr, acc_scr, carry_scr, *, tq, n_group, scale):
    qt = pl.program_id(2)
    for g in range(n_group):
        qs_scr[g * tq:(g + 1) * tq, :] = q_ref[:, g * HEAD_DIM:(g + 1) * HEAD_DIM].astype(BF16)
    u2 = _cumsum_matrix(tq)
    rows = n_group * tq

    def tile(kt, masked):
        start = pl.multiple_of(kt * tq, tq)
        k = k_ref[pl.ds(start, tq), :].astype(BF16)
        v = v_ref[pl.ds(start, tq), :].astype(BF16)
        z = _nt(qs_scr[...], k) * scale
        past = None
        if masked:
            r = lax.broadcasted_iota(jnp.int32, (rows, tq), 0)
            c = lax.broadcasted_iota(jnp.int32, (rows, tq), 1)
            past = c < _imod(r, tq)
        w, tot = _sb_tile(z, past, carry_scr[...], u2)
        acc_scr[...] += _nn(w.astype(BF16), v)
        carry_scr[...] += tot

    acc_scr[...] = jnp.zeros_like(acc_scr)
    carry_scr[...] = jnp.zeros_like(carry_scr)
    tile(qt, True)

    def cond(j):
        return jnp.logical_and(j < qt, _sb_live(carry_scr[...]))

    def body(j):
        tile(qt - 1 - j, False)
        return j + 1

    lax.while_loop(cond, body, 0)
    for g in range(n_group):
        o_ref[:, g * HEAD_DIM:(g + 1) * HEAD_DIM] = acc_scr[g * tq:(g + 1) * tq, :].astype(o_ref.dtype)


def sb_prompt(qkv, *, B, T, n_heads, n_kv, tq=128):
    n_group = n_heads // n_kv
    nqt = T // tq
    gw = n_group * HEAD_DIM
    body = functools.partial(_sb_prompt_body, tq=tq, n_group=n_group, scale=HEAD_DIM ** -0.5)
    return pl.pallas_call(
        body,
        grid=(B, n_kv, nqt),
        in_specs=[pl.BlockSpec((tq, gw), lambda b, kh, qt: (b * nqt + qt, kh)),
                  pl.BlockSpec((T, HEAD_DIM), lambda b, kh, qt: (b, n_heads + kh)),
                  pl.BlockSpec((T, HEAD_DIM), lambda b, kh, qt: (b, n_heads + n_kv + kh))],
        out_specs=pl.BlockSpec((tq, gw), lambda b, kh, qt: (b * nqt + qt, kh)),
        out_shape=jax.ShapeDtypeStruct((B * T, n_heads * HEAD_DIM), BF16),
        scratch_shapes=[pltpu.VMEM((n_group * tq, HEAD_DIM), BF16), pltpu.VMEM((n_group * tq, HEAD_DIM), F32),
                        pltpu.VMEM((n_group * tq, 1), F32)],
        compiler_params=_params("parallel", "parallel", "arbitrary"),
        name="sb_prompt",
    )(qkv, qkv, qkv)


def _page_specs(layer, n_pages, P, page_shape, reverse):
    def spec(i):
        def imap(b, c, pt):
            p = c * P + i
            if reverse:
                p = n_pages - 1 - p
            return (layer, pt[b, p]) + (0,) * len(page_shape)
        return pl.BlockSpec((None, None) + tuple(page_shape), imap)
    return [spec(i) for i in range(P)]


def _page_heads(ref, n_kv):
    page = ref.shape[0] // n_kv
    return [ref[pl.ds(h, page, stride=n_kv), :].astype(BF16) for h in range(n_kv)]


def _lane_heads(x, n_kv):
    return [x[:, h * HEAD_DIM:(h + 1) * HEAD_DIM].astype(BF16) for h in range(n_kv)]


def _sb_sample_body(pt_ref, q_ref, kn_ref, vn_ref, k_hbm, v_hbm, o_ref, acc_scr, carry_scr, k_buf, v_buf, sem, *,
                    P, n_pages, layer, n_kv, t_new, scale):
    b = pl.program_id(0)
    q = q_ref[...].astype(BF16)
    rows = q.shape[0]
    rk = rows // n_kv
    u2 = _cumsum_matrix(LANES)

    def tile(ks, vs, masked, carry, acc):
        z = jnp.concatenate([_nt(q[h * rk:(h + 1) * rk], ks[h]) for h in range(n_kv)], axis=0) * scale
        past = None
        if masked:
            r = lax.broadcasted_iota(jnp.int32, z.shape, 0)
            col = lax.broadcasted_iota(jnp.int32, z.shape, 1)
            past = col < _imod(r, t_new)
        w, tot = _sb_tile(z, past, carry, u2)
        wb = w.astype(BF16)
        pv = jnp.concatenate([_nn(wb[h * rk:(h + 1) * rk], vs[h]) for h in range(n_kv)], axis=0)
        return carry + tot, acc + pv

    carry, acc = tile(_lane_heads(kn_ref[...], n_kv), _lane_heads(vn_ref[...], n_kv), True,
                      jnp.zeros(carry_scr.shape, F32), jnp.zeros(acc_scr.shape, F32))
    carry_scr[...] = carry
    acc_scr[...] = acc

    def page_copy(cache_hbm, buf, i, page):
        return pltpu.make_async_copy(cache_hbm.at[layer, page], buf.at[i], sem)

    def cond(c):
        return jnp.logical_and(c < n_pages // P, _sb_live(carry_scr[...]))

    def body(c):
        for i in range(P):
            page = pt_ref[b, n_pages - 1 - (c * P + i)]
            page_copy(k_hbm, k_buf, i, page).start()
            page_copy(v_hbm, v_buf, i, page).start()
        for i in range(P):
            page_copy(k_hbm, k_buf, i, 0).wait()
            page_copy(v_hbm, v_buf, i, 0).wait()
        carry, acc = carry_scr[...], acc_scr[...]
        for i in range(P):
            carry, acc = tile(_page_heads(k_buf.at[i], n_kv), _page_heads(v_buf.at[i], n_kv), False, carry, acc)
        carry_scr[...] = carry
        acc_scr[...] = acc
        return c + 1

    lax.while_loop(cond, body, 0)
    o_ref[...] = acc_scr[...].astype(o_ref.dtype)


def sb_sample(q, k_new, v_new, cache_k, cache_v, layer, page_table, *, n_kv, t_new, P):
    B, rows, _ = q.shape
    n_pages = page_table.shape[1]
    P = _fit(n_pages, P)
    kvw = n_kv * HEAD_DIM
    page_shape = cache_k.shape[2:]
    body = functools.partial(_sb_sample_body, P=P, n_pages=n_pages, layer=layer, n_kv=n_kv, t_new=t_new,
                             scale=HEAD_DIM ** -0.5)
    bspec = lambda w: pl.BlockSpec((None, rows if w == HEAD_DIM else LANES, w), lambda b, pt: (b, 0, 0))
    grid_spec = pltpu.PrefetchScalarGridSpec(
        num_scalar_prefetch=1,
        grid=(B,),
        in_specs=[bspec(HEAD_DIM), bspec(kvw), bspec(kvw),
                  pl.BlockSpec(memory_space=pl.ANY), pl.BlockSpec(memory_space=pl.ANY)],
        out_specs=pl.BlockSpec((None, rows, HEAD_DIM), lambda b, pt: (b, 0, 0)),
        scratch_shapes=[pltpu.VMEM((rows, HEAD_DIM), F32), pltpu.VMEM((rows, 1), F32),
                        pltpu.VMEM((P,) + page_shape, F32), pltpu.VMEM((P,) + page_shape, F32),
                        pltpu.SemaphoreType.DMA(())],
    )
    return pl.pallas_call(
        body,
        grid_spec=grid_spec,
        out_shape=jax.ShapeDtypeStruct((B, rows, HEAD_DIM), BF16),
        compiler_params=_params("arbitrary"),
        name="sb_sample",
    )(page_table, q, k_new, v_new, cache_k, cache_v)


def _softmax_step(s_list, m_scr, l_scr):
    m_old = m_scr[...]
    m_new = m_old
    for s in s_list:
        m_new = jnp.maximum(m_new, jnp.max(s, axis=1, keepdims=True))
    alpha = jnp.exp(m_old - m_new)
    p_list = [jnp.exp(s - m_new) for s in s_list]
    l = alpha * l_scr[...]
    for p in p_list:
        l = l + jnp.sum(p, axis=1, keepdims=True)
    m_scr[...] = m_new
    l_scr[...] = l
    return alpha, p_list


def _softmax_init(m_scr, l_scr, acc_scr):
    m_scr[...] = jnp.full(m_scr.shape, NEG, F32)
    l_scr[...] = jnp.zeros_like(l_scr)
    acc_scr[...] = jnp.zeros_like(acc_scr)


def _mla_prompt_body(q_ref, kn_ref, kr_ref, v_ref, o_ref, m_scr, l_scr, acc_scr, *, tq, tk, scale):
    qt = pl.program_id(2)
    qn = q_ref[:, 0:HEAD_DIM]
    qr = q_ref[:, HEAD_DIM:2 * HEAD_DIM]
    _softmax_init(m_scr, l_scr, acc_scr)

    def chunk(ct, masked):
        start = pl.multiple_of(ct * tk, tk)
        kn = kn_ref[pl.ds(start, tk), :]
        kr = kr_ref[pl.ds(start, tk), :].astype(BF16)
        v = v_ref[pl.ds(start, tk), :]
        s = (_nt(qn, kn) + _nt(qr, kr)) * scale
        if masked:
            r = lax.broadcasted_iota(jnp.int32, s.shape, 0)
            c = lax.broadcasted_iota(jnp.int32, s.shape, 1)
            s = jnp.where(ct * tk + c <= qt * tq + r, s, NEG)
        alpha, (p,) = _softmax_step([s], m_scr, l_scr)
        acc_scr[...] = alpha * acc_scr[...] + _nn(p.astype(BF16), v)

    def body(ct, _):
        chunk(ct, False)
        return 0

    n_full = (qt * tq) // tk
    lax.fori_loop(0, n_full, body, 0)
    chunk(n_full, True)
    o_ref[...] = (acc_scr[...] / l_scr[...]).astype(o_ref.dtype)


def mla_prompt(q_cat, kv_up, down, *, B, T, n_heads, kr_block, tq=256, tk=1024):
    tq = _fit(T, tq)
    tk = _fit(T, tk)
    assert tk % tq == 0
    nqt = T // tq
    body = functools.partial(_mla_prompt_body, tq=tq, tk=tk, scale=(HEAD_DIM + HEAD_DIM // 2) ** -0.5)
    return pl.pallas_call(
        body,
        grid=(B, n_heads, nqt),
        in_specs=[pl.BlockSpec((tq, 2 * HEAD_DIM), lambda b, h, qt: (b * nqt + qt, h)),
                  pl.BlockSpec((T, HEAD_DIM), lambda b, h, qt: (b, h)),
                  pl.BlockSpec((T, LANES), lambda b, h, qt: (b, kr_block)),
                  pl.BlockSpec((T, HEAD_DIM), lambda b, h, qt: (b, n_heads + h))],
        out_specs=pl.BlockSpec((tq, HEAD_DIM), lambda b, h, qt: (b * nqt + qt, h)),
        out_shape=jax.ShapeDtypeStruct((B * T, n_heads * HEAD_DIM), BF16),
        scratch_shapes=[pltpu.VMEM((tq, 1), F32), pltpu.VMEM((tq, 1), F32), pltpu.VMEM((tq, HEAD_DIM), F32)],
        compiler_params=_params("parallel", "parallel", "arbitrary"),
        name="mla_prompt",
    )(q_cat, kv_up, down, kv_up)


def _mla_sample_body(pt_ref, ql_ref, qc_ref, new_ref, *rest, P, kv_lora, rope_dim, n_heads, scale):
    c_refs, r_refs = rest[:P], rest[P:2 * P]
    o_ref, m_scr, l_scr, acc_scr = rest[2 * P:2 * P + 4]
    c = pl.program_id(1)
    ql = ql_ref[...]
    qr = qc_ref[:, HEAD_DIM:HEAD_DIM + rope_dim]

    @pl.when(c == 0)
    def _():
        _softmax_init(m_scr, l_scr, acc_scr)
        ck = new_ref[:, 0:kv_lora].astype(BF16)
        kr = new_ref[:, kv_lora:kv_lora + rope_dim].astype(BF16)
        s = (_nt(ql, ck) + _nt(qr, kr)) * scale
        r = lax.broadcasted_iota(jnp.int32, s.shape, 0)
        col = lax.broadcasted_iota(jnp.int32, s.shape, 1)
        s = jnp.where(col <= _idiv(r, n_heads), s, NEG)
        alpha, (p,) = _softmax_step([s], m_scr, l_scr)
        acc_scr[...] = alpha * acc_scr[...] + _nn(p.astype(BF16), ck)

    cks = [c_refs[i][...].astype(BF16) for i in range(P)]
    s_list = [(_nt(ql, cks[i]) + _nn(qr, r_refs[i][...].astype(BF16))) * scale for i in range(P)]
    alpha, p_list = _softmax_step(s_list, m_scr, l_scr)
    acc = alpha * acc_scr[...]
    for i in range(P):
        acc = acc + _nn(p_list[i].astype(BF16), cks[i])
    acc_scr[...] = acc

    @pl.when(c == pl.num_programs(1) - 1)
    def _():
        o_ref[...] = (acc_scr[...] / l_scr[...]).astype(o_ref.dtype)


def mla_sample(q_lat, q_cat, new, cache_ckv, cache_kpe, layer, page_table, *, n_heads, P):
    B, rows, kv_lora = q_lat.shape
    rope_dim = cache_kpe.shape[2]
    n_pages = page_table.shape[1]
    P = _fit(n_pages, P)
    body = functools.partial(_mla_sample_body, P=P, kv_lora=kv_lora, rope_dim=rope_dim, n_heads=n_heads,
                             scale=(HEAD_DIM + rope_dim) ** -0.5)
    bspec = lambda r, w: pl.BlockSpec((None, r, w), lambda b, c, pt: (b, 0, 0))
    grid_spec = pltpu.PrefetchScalarGridSpec(
        num_scalar_prefetch=1,
        grid=(B, n_pages // P),
        in_specs=[bspec(rows, kv_lora), bspec(rows, 2 * HEAD_DIM), bspec(LANES, new.shape[-1])]
        + _page_specs(layer, n_pages, P, (LANES, kv_lora), False)
        + _page_specs(layer, n_pages, P, (rope_dim, LANES), False),
        out_specs=bspec(rows, kv_lora),
        scratch_shapes=[pltpu.VMEM((rows, 1), F32), pltpu.VMEM((rows, 1), F32), pltpu.VMEM((rows, kv_lora), F32)],
    )
    return pl.pallas_call(
        body,
        grid_spec=grid_spec,
        out_shape=jax.ShapeDtypeStruct((B, rows, kv_lora), BF16),
        compiler_params=_params("parallel", "arbitrary"),
        name="mla_sample",
    )(page_table, q_lat, q_cat, new, *([cache_ckv] * P), *([cache_kpe] * P))


def _top_blocks(gate, n_valid, n_sel):
    lane = lax.broadcasted_iota(jnp.int32, gate.shape, 1)
    g = jnp.where(lane < n_valid, gate, NEG)
    lane = lane.astype(F32)
    sel = jnp.zeros(gate.shape, F32)
    for r in range(n_sel):
        m = jnp.max(g, axis=1, keepdims=True)
        idx = jnp.min(jnp.where(g == m, lane, float(LANES)), axis=1, keepdims=True)
        pick = lane == idx
        sel = jnp.where(jnp.logical_and(pick, n_valid > r), 1.0, sel)
        g = jnp.where(pick, -jnp.inf, g)
    return sel


def _sel_column(sel, n):
    lane = lax.broadcasted_iota(jnp.int32, sel.shape, 1)
    return jnp.sum(jnp.where(lane == n, sel, 0.0), axis=1, keepdims=True) > 0.0


def _moba_prompt_body(q_ref, k_ref, v_ref, o_ref, km_scr, qb_scr, sel_scr, m_scr, l_scr, acc_scr, *, tq, n_group,
                      n_blocks, scale):
    qt = pl.program_id(2)

    @pl.when(qt == 0)
    def _():
        km_scr[...] = jnp.zeros_like(km_scr)
        for n in range(n_blocks):
            blk = k_ref[n * MOBA_BLOCK:(n + 1) * MOBA_BLOCK, :]
            km_scr[n:n + 1, :] = jnp.sum(blk, axis=0, keepdims=True) * (1.0 / MOBA_BLOCK)

    own = _idiv(qt * tq, MOBA_BLOCK)
    km = km_scr[...]
    for g in range(n_group):
        q32 = q_ref[:, g * HEAD_DIM:(g + 1) * HEAD_DIM]
        qb_scr[g] = q32.astype(BF16)
        sel_scr[g] = _top_blocks(_nt3(q32, km), own, MOBA_TOPK).astype(BF16)
    _softmax_init(m_scr, l_scr, acc_scr)

    def tile(n, causal):
        start = pl.multiple_of(n * MOBA_BLOCK, MOBA_BLOCK)
        k = k_ref[pl.ds(start, MOBA_BLOCK), :].astype(BF16)
        v = v_ref[pl.ds(start, MOBA_BLOCK), :].astype(BF16)
        if causal is None:
            j = lax.broadcasted_iota(jnp.int32, (LANES, MOBA_BLOCK), 0)
            spread = jnp.where(j == n, 1.0, 0.0).astype(BF16)
        heads = range(n_group)
        raw = [_nt(qb_scr[g], k) for g in heads]
        valid = [causal if causal is not None else _nn(sel_scr[g], spread) > 0.5 for g in heads]
        s = [jnp.where(valid[g], raw[g] * scale, NEG) for g in heads]
        m_old = [m_scr[g] for g in heads]
        m_new = [jnp.maximum(m_old[g], jnp.max(s[g], axis=1, keepdims=True)) for g in heads]
        alpha = [jnp.exp(m_old[g] - m_new[g]) for g in heads]
        p = [jnp.exp(s[g] - m_new[g]) for g in heads]
        pv = [_nn(p[g].astype(BF16), v) for g in heads]
        for g in heads:
            l_scr[g] = alpha[g] * l_scr[g] + jnp.sum(p[g], axis=1, keepdims=True)
            m_scr[g] = m_new[g]
            acc_scr[g] = alpha[g] * acc_scr[g] + pv[g]

    def body(n, _):
        tile(n, None)
        return 0

    lax.fori_loop(0, own, body, 0)
    r = lax.broadcasted_iota(jnp.int32, (tq, MOBA_BLOCK), 0)
    c = lax.broadcasted_iota(jnp.int32, (tq, MOBA_BLOCK), 1)
    tile(own, own * MOBA_BLOCK + c <= qt * tq + r)
    for g in range(n_group):
        o_ref[:, g * HEAD_DIM:(g + 1) * HEAD_DIM] = (acc_scr[g] / l_scr[g]).astype(o_ref.dtype)


def moba_prompt(qkv, *, B, T, n_heads, n_kv, tq=128):
    n_group = n_heads // n_kv
    assert MOBA_BLOCK % tq == 0 and T % MOBA_BLOCK == 0 and T // MOBA_BLOCK <= LANES
    nqt = T // tq
    gw = n_group * HEAD_DIM
    rows = n_group * tq
    body = functools.partial(_moba_prompt_body, tq=tq, n_group=n_group, n_blocks=T // MOBA_BLOCK,
                             scale=HEAD_DIM ** -0.5)
    return pl.pallas_call(
        body,
        grid=(B, n_kv, nqt),
        in_specs=[pl.BlockSpec((tq, gw), lambda b, kh, qt: (b * nqt + qt, kh)),
                  pl.BlockSpec((T, HEAD_DIM), lambda b, kh, qt: (b, n_heads + kh)),
                  pl.BlockSpec((T, HEAD_DIM), lambda b, kh, qt: (b, n_heads + n_kv + kh))],
        out_specs=pl.BlockSpec((tq, gw), lambda b, kh, qt: (b * nqt + qt, kh)),
        out_shape=jax.ShapeDtypeStruct((B * T, n_heads * HEAD_DIM), BF16),
        scratch_shapes=[pltpu.VMEM((LANES, HEAD_DIM), F32), pltpu.VMEM((n_group, tq, HEAD_DIM), BF16),
                        pltpu.VMEM((n_group, tq, LANES), BF16), pltpu.VMEM((n_group, tq, 1), F32),
                        pltpu.VMEM((n_group, tq, 1), F32), pltpu.VMEM((n_group, tq, HEAD_DIM), F32)],
        compiler_params=_params("arbitrary", "arbitrary", "arbitrary"),
        name="moba_prompt",
    )(qkv, qkv, qkv)


def _moba_select_body(pt_ref, q_ref, *rest, P, n_kv, n_past_blocks):
    k_refs = rest[:P]
    sel_ref, km_scr = rest[P], rest[P + 1]
    c = pl.program_id(1)
    pages_per_block = MOBA_BLOCK // LANES

    @pl.when(c == 0)
    def _():
        km_scr[...] = jnp.zeros_like(km_scr)

    row = lax.broadcasted_iota(jnp.int32, km_scr.shape, 0)
    km = km_scr[...]
    for i in range(P // pages_per_block):
        means = []
        for h in range(n_kv):
            tot = None
            for j in range(pages_per_block):
                page = k_refs[pages_per_block * i + j][pl.ds(h, LANES, stride=n_kv), :]
                tot = page if tot is None else tot + page
            means.append(jnp.sum(tot, axis=0, keepdims=True) * (1.0 / MOBA_BLOCK))
        mean = jnp.concatenate(means, axis=1)
        km = jnp.where(row == c * (P // pages_per_block) + i, mean, km)
    km_scr[...] = km

    @pl.when(c == pl.num_programs(1) - 1)
    def _():
        q = q_ref[...]
        rk = q.shape[0] // n_kv
        gate = jnp.concatenate([_nt3(q[h * rk:(h + 1) * rk], km[:, h * HEAD_DIM:(h + 1) * HEAD_DIM])
                                for h in range(n_kv)], axis=0)
        sel_ref[...] = _top_blocks(gate, n_past_blocks, MOBA_TOPK)


def moba_select(q, cache_k, layer, page_table, *, n_kv, P):
    B, rows, _ = q.shape
    n_pages = page_table.shape[1]
    P = _fit(n_pages, P)
    kvw = n_kv * HEAD_DIM
    n_past_blocks = n_pages * LANES // MOBA_BLOCK
    assert n_past_blocks <= LANES and P % (MOBA_BLOCK // LANES) == 0
    body = functools.partial(_moba_select_body, P=P, n_kv=n_kv, n_past_blocks=n_past_blocks)
    grid_spec = pltpu.PrefetchScalarGridSpec(
        num_scalar_prefetch=1,
        grid=(B, n_pages // P),
        in_specs=[pl.BlockSpec((None, rows, HEAD_DIM), lambda b, c, pt: (b, 0, 0))]
        + _page_specs(layer, n_pages, P, cache_k.shape[2:], False),
        out_specs=pl.BlockSpec((None, rows, LANES), lambda b, c, pt: (b, 0, 0)),
        scratch_shapes=[pltpu.VMEM((LANES, kvw), F32)],
    )
    return pl.pallas_call(
        body,
        grid_spec=grid_spec,
        out_shape=jax.ShapeDtypeStruct((B, rows, LANES), F32),
        compiler_params=_params("parallel", "arbitrary"),
        name="moba_select",
    )(page_table, q, *([cache_k] * P))


def _moba_sample_body(pt_ref, q_ref, sel_ref, kn_ref, vn_ref, *rest, P, n_kv, t_new, scale):
    k_refs, v_refs = rest[:P], rest[P:2 * P]
    o_ref, m_scr, l_scr, acc_scr = rest[2 * P:2 * P + 4]
    c = pl.program_id(1)
    q = q_ref[...].astype(BF16)
    rk = q.shape[0] // n_kv
    pages_per_block = MOBA_BLOCK // LANES

    def scores(ks):
        return jnp.concatenate([_nt(q[h * rk:(h + 1) * rk], ks[h]) for h in range(n_kv)], axis=0) * scale

    def pv(acc, p, vs):
        pb = p.astype(BF16)
        return acc + jnp.concatenate([_nn(pb[h * rk:(h + 1) * rk], vs[h]) for h in range(n_kv)], axis=0)

    @pl.when(c == 0)
    def _():
        _softmax_init(m_scr, l_scr, acc_scr)
        s = scores(_lane_heads(kn_ref[...], n_kv))
        r = lax.broadcasted_iota(jnp.int32, s.shape, 0)
        col = lax.broadcasted_iota(jnp.int32, s.shape, 1)
        s = jnp.where(col <= _imod(r, t_new), s, NEG)
        alpha, (p,) = _softmax_step([s], m_scr, l_scr)
        acc_scr[...] = pv(alpha * acc_scr[...], p, _lane_heads(vn_ref[...], n_kv))

    sel = sel_ref[...]
    s_list = []
    for i in range(P):
        valid = _sel_column(sel, _idiv(c * P + i, pages_per_block))
        s_list.append(jnp.where(valid, scores(_page_heads(k_refs[i], n_kv)), NEG))
    alpha, p_list = _softmax_step(s_list, m_scr, l_scr)
    acc = alpha * acc_scr[...]
    for i in range(P):
        acc = pv(acc, p_list[i], _page_heads(v_refs[i], n_kv))
    acc_scr[...] = acc

    @pl.when(c == pl.num_programs(1) - 1)
    def _():
        o_ref[...] = (acc_scr[...] / l_scr[...]).astype(o_ref.dtype)


def moba_sample(q, sel, k_new, v_new, cache_k, cache_v, layer, page_table, *, n_kv, t_new, P):
    B, rows, _ = q.shape
    n_pages = page_table.shape[1]
    P = _fit(n_pages, P)
    kvw = n_kv * HEAD_DIM
    body = functools.partial(_moba_sample_body, P=P, n_kv=n_kv, t_new=t_new, scale=HEAD_DIM ** -0.5)
    bspec = lambda r, w: pl.BlockSpec((None, r, w), lambda b, c, pt: (b, 0, 0))
    grid_spec = pltpu.PrefetchScalarGridSpec(
        num_scalar_prefetch=1,
        grid=(B, n_pages // P),
        in_specs=[bspec(rows, HEAD_DIM), bspec(rows, LANES), bspec(LANES, kvw), bspec(LANES, kvw)]
        + _page_specs(layer, n_pages, P, cache_k.shape[2:], False)
        + _page_specs(layer, n_pages, P, cache_v.shape[2:], False),
        out_specs=bspec(rows, HEAD_DIM),
        scratch_shapes=[pltpu.VMEM((rows, 1), F32), pltpu.VMEM((rows, 1), F32), pltpu.VMEM((rows, HEAD_DIM), F32)],
    )
    return pl.pallas_call(
        body,
        grid_spec=grid_spec,
        out_shape=jax.ShapeDtypeStruct((B, rows, HEAD_DIM), BF16),
        compiler_params=_params("parallel", "arbitrary"),
        name="moba_sample",
    )(page_table, q, sel, k_new, v_new, *([cache_k] * P), *([cache_v] * P))


def _heads_first(x, B, t_new, n_heads, width):
    return x.reshape(t_new, B, n_heads, width).transpose(1, 2, 0, 3).reshape(B, n_heads * t_new, width)


def _heads_last(o, B, t_new, n_heads, width):
    return o.reshape(B, n_heads, t_new, width).transpose(2, 0, 1, 3).reshape(t_new * B, n_heads * width)


def _batch_major(x, B, t_new):
    return x.reshape(t_new, B, x.shape[-1]).transpose(1, 0, 2)


def _time_major(x, B, t_new):
    k = x.shape[1] // t_new
    return x.reshape(B, t_new, k * x.shape[-1]).transpose(1, 0, 2).reshape(t_new * B, k * x.shape[-1])


def _pad_new(x_bt):
    return jnp.pad(x_bt, ((0, 0), (0, LANES - x_bt.shape[1]), (0, 0)))


TM_PROMPT = 1024
TM_FFN = 512
TN = 512
PAGES_PER_STEP = 8
PAGES_PER_SELECT_STEP = 16
SB_PAGES_PER_FETCH = 4


def kernel(x_prompt, x_sample, cache_sb_k, cache_sb_v, cache_mla_ckv, cache_mla_kpe, cache_moba_k, cache_moba_v, page_table, c_prompt, c_sample, mod_w, mod_b, norm_mix, norm_ffn, final_norm, sb_wq, sb_wk, sb_wv, sb_wo, mla_wdq, mla_q_norm, mla_wuq, mla_wdkv, mla_kv_norm, mla_wuk, mla_wuv, mla_wo, moba_wq, moba_wk, moba_wv, moba_wo, ffn_wg, ffn_wu, ffn_wd, moe_wr, moe_br, moe_wg, moe_wu, moe_wd):
    Bp, T, D = x_prompt.shape
    Bs, Ts, _ = x_sample.shape
    depth = mod_w.shape[0]
    n_pages, page = page_table.shape[1], cache_sb_k.shape[2]
    past_len = n_pages * page
    assert page == LANES and Ts == SUBLANES and past_len % MOBA_BLOCK == 0
    n_heads = sb_wq.shape[2] // HEAD_DIM
    sb_kv = sb_wk.shape[2] // HEAD_DIM
    moba_kv = moba_wk.shape[2] // HEAD_DIM
    q_lora, kv_lora = mla_wdq.shape[2], mla_kv_norm.shape[1]
    rope_dim = mla_wdkv.shape[2] - kv_lora
    nope = mla_wuk.shape[3]
    assert nope == HEAD_DIM and mla_wuv.shape[3] == HEAD_DIM and rope_dim == HEAD_DIM // 2
    pos_p = jnp.arange(T, dtype=jnp.int32)
    pos_s = past_len + jnp.arange(Ts, dtype=jnp.int32)
    streams = ((Bp, T, Bp, T, pos_p, TM_PROMPT), (Ts, Bs, Bs, Ts, jnp.repeat(pos_s, Bs), Ts * Bs))

    n_c = Bs + Bp
    c_all = jnp.concatenate([c_sample, c_prompt, jnp.zeros((-n_c % SUBLANES, D), F32)], axis=0)
    mod_all = modulation_all(c_all, mod_w, mod_b).reshape(depth, c_all.shape[0], 6, D)

    flat_pages = lambda c: c.reshape(c.shape[0], c.shape[1], c.shape[2] * c.shape[3], c.shape[4])
    sb_cache_k, sb_cache_v = flat_pages(cache_sb_k), flat_pages(cache_sb_v)
    mb_cache_k, mb_cache_v = flat_pages(cache_moba_k), flat_pages(cache_moba_v)
    cache_kpe_t = cache_mla_kpe.transpose(0, 1, 3, 2)

    xs = [x_prompt, x_sample.transpose(1, 0, 2)]
    outs = {k: ([], []) for k in ("sb_k", "sb_v", "mla_c", "mla_r", "mb_k", "mb_v")}
    for i in range(depth):
        kind, j, f = i % N_MIXERS, i // N_MIXERS, i // 2
        mods = (mod_all[i, Bs:Bs + Bp].transpose(1, 0, 2).reshape(6, Bp, 1, D),
                mod_all[i, :Bs].transpose(1, 0, 2).reshape(6, 1, Bs, D))
        for s, (G, R, B, Tn, pos, tm) in enumerate(streams):
            x3, mod = xs[s], mods[s]
            is_sample = s == 1
            to_bt = (lambda y: _batch_major(y, B, Tn)) if is_sample else (lambda y: y.reshape(B, Tn, y.shape[-1]))
            if kind == 0:
                w = jnp.concatenate([sb_wq[j], sb_wk[j], sb_wv[j]], axis=1)
                qkv = ada_linear(x3, mod, 0, 1, norm_mix[i], w, tm=tm, tn=TN, name="sb_qkv")
                qw, kw = n_heads * HEAD_DIM, sb_kv * HEAD_DIM
                k_new, v_new = to_bt(qkv[:, qw:qw + kw]), to_bt(qkv[:, qw + kw:])
                outs["sb_k"][s].append(k_new.reshape(B, Tn, sb_kv, HEAD_DIM))
                outs["sb_v"][s].append(v_new.reshape(B, Tn, sb_kv, HEAD_DIM))
                if is_sample:
                    q = _heads_first(qkv[:, :qw], B, Tn, n_heads, HEAD_DIM)
                    o = sb_sample(q, _pad_new(k_new), _pad_new(v_new), sb_cache_k, sb_cache_v, j,
                                  page_table, n_kv=sb_kv, t_new=Tn, P=SB_PAGES_PER_FETCH)
                    o = _heads_last(o, B, Tn, n_heads, HEAD_DIM)
                else:
                    o = sb_prompt(qkv, B=B, T=Tn, n_heads=n_heads, n_kv=sb_kv)
                wo = sb_wo[j]
            elif kind == 1:
                half = rope_dim // 2
                c64, s1, s2 = _rope_tables(pos, rope_dim)
                w = jnp.concatenate([mla_wdq[j], mla_wdkv[j], jnp.zeros((D, LANES - rope_dim), F32)], axis=1)
                tmd = min(tm, TM_FFN)
                tabs, tab_specs = _row_tables((c64, s1, s2), G, R, tmd)
                down = ada_linear(
                    x3, mod, 0, 1, norm_mix[i], w, tm=tmd, tn=w.shape[1],
                    epilogue=functools.partial(_epi_mla_down, q_lora=q_lora, kv_lora=kv_lora, half=half),
                    extra=(mla_q_norm[j].reshape(1, q_lora), mla_kv_norm[j].reshape(1, kv_lora)) + tabs,
                    extra_specs=[_const_spec(q_lora), _const_spec(kv_lora)] + tab_specs,
                    name="mla_down")
                ckv = to_bt(down[:, q_lora:q_lora + kv_lora])
                kpe = to_bt(down[:, q_lora + kv_lora:q_lora + kv_lora + rope_dim])
                outs["mla_c"][s].append(ckv)
                outs["mla_r"][s].append(kpe)
                wuq = mla_wuq[j].reshape(q_lora, n_heads, nope + rope_dim)
                wuq = jnp.pad(wuq, ((0, 0), (0, 0), (0, 2 * HEAD_DIM - nope - rope_dim))).reshape(q_lora, -1)
                gb, rb, _ = _row_tiling(G, R, tm)
                tabs, tab_specs = _row_tables((c64, s1, s2), G, R, tm)
                q_cat = linear(down, wuq, x_cols=(q_lora, 0), tm=gb * rb, tn=TN, out_dtype=BF16,
                               epilogue=functools.partial(_epi_q_up, half=half), extra=tabs, extra_specs=tab_specs,
                               name="mla_q_up")
                wuk2 = mla_wuk[j].reshape(kv_lora, n_heads * nope)
                wuv2 = mla_wuv[j].reshape(kv_lora, n_heads * HEAD_DIM)
                if is_sample:
                    q_lat = head_linear(q_cat, wuk2, n_heads=n_heads, x_width=HEAD_DIM, x_stride=2, x_off=0,
                                        w_width=HEAD_DIM, transpose_w=True, out_width=kv_lora, tm=tm, name="mla_q_lat")
                    per_batch = lambda y, wd: _batch_major(y, B, Tn).reshape(B, Tn * n_heads, wd)
                    new = _pad_new(_batch_major(down[:, q_lora:], B, Tn))
                    o_lat = mla_sample(per_batch(q_lat, kv_lora), per_batch(q_cat, 2 * HEAD_DIM), new,
                                       cache_mla_ckv, cache_kpe_t, j, page_table, n_heads=n_heads,
                                       P=PAGES_PER_STEP)
                    o = head_linear(_time_major(o_lat, B, Tn), wuv2, n_heads=n_heads, x_width=kv_lora,
                                    x_stride=1, x_off=0, w_width=HEAD_DIM, transpose_w=False, out_width=HEAD_DIM,
                                    tm=tm, name="mla_o_up")
                else:
                    assert q_lora % kv_lora == 0
                    kv_up = linear(down, jnp.concatenate([wuk2, wuv2], axis=1), x_cols=(kv_lora, q_lora // kv_lora),
                                   tm=tm, tn=TN, out_dtype=BF16, name="mla_kv_up")
                    o = mla_prompt(q_cat, kv_up, down, B=B, T=Tn, n_heads=n_heads,
                                   kr_block=(q_lora + kv_lora) // LANES)
                wo = mla_wo[j]
            else:
                c128, s1, s2 = _rope_tables(pos, HEAD_DIM)
                tabs, tab_specs = _row_tables((c128, s1 + s2), G, R, tm)
                w = jnp.concatenate([moba_wq[j], moba_wk[j], moba_wv[j]], axis=1)
                qw, kw = n_heads * HEAD_DIM, moba_kv * HEAD_DIM
                tn = 2 * LANES
                qkv = ada_linear(x3, mod, 0, 1, norm_mix[i], w, tm=tm, tn=tn,
                                 epilogue=functools.partial(_epi_rope128, n_rope_tiles=(qw + kw) // tn),
                                 extra=tabs, extra_specs=tab_specs, name="moba_qkv")
                k_new, v_new = to_bt(qkv[:, qw:qw + kw]), to_bt(qkv[:, qw + kw:])
                outs["mb_k"][s].append(k_new.reshape(B, Tn, moba_kv, HEAD_DIM))
                outs["mb_v"][s].append(v_new.reshape(B, Tn, moba_kv, HEAD_DIM))
                if is_sample:
                    q = _heads_first(qkv[:, :qw], B, Tn, n_heads, HEAD_DIM)
                    sel = moba_select(q, mb_cache_k, j, page_table, n_kv=moba_kv, P=PAGES_PER_SELECT_STEP)
                    o = moba_sample(q, sel, _pad_new(k_new), _pad_new(v_new), mb_cache_k, mb_cache_v, j,
                                    page_table, n_kv=moba_kv, t_new=Tn, P=PAGES_PER_STEP)
                    o = _heads_last(o, B, Tn, n_heads, HEAD_DIM)
                else:
                    o = moba_prompt(qkv, B=B, T=Tn, n_heads=n_heads, n_kv=moba_kv)
                wo = moba_wo[j]
            x3 = linear_res(o, wo, x3, mod, 2, tm=tm, tn=TN, name="mixer_out")
            tmf = min(tm, TM_FFN)
            if i % 2 == 0:
                x3 = ffn_dense(x3, mod, norm_ffn[i], ffn_wg, ffn_wu, ffn_wd, f, tm=tmf, tf=256)
            else:
                x3 = moe_top2(x3, mod, norm_ffn[i], moe_wr[f], moe_br[f], moe_wg, moe_wu, moe_wd, f,
                              tm_route=tmf, tm_group=min(TM_FFN, max(G * R // 4, LANES)), tf=256)
            xs[s] = x3
    y_prompt = rms_final(xs[0], final_norm, tm=TM_FFN)
    y_sample = rms_final(xs[1], final_norm, tm=TM_FFN).transpose(1, 0, 2)
    st = lambda key, s: jnp.stack(outs[key][s])
    return (y_prompt, y_sample,
            st("sb_k", 0), st("sb_v", 0), st("sb_k", 1), st("sb_v", 1),
            st("mla_c", 0), st("mla_r", 0), st("mla_c", 1), st("mla_r", 1),
            st("mb_k", 0), st("mb_v", 0), st("mb_k", 1), st("mb_v", 1))
```
